```python
import math
import jax
import jax.numpy as jnp
from jax import lax
import numpy as np


D_MODEL = 1024
BATCH = 2
SEQ = 16384
DEPTH = 2
DEC_BATCH = 16
DEC_SEQ = 2048
PAST_LEN = 128

PLE_DIM = 256
D_FF = 2816
ROPE_THETA = 10000.0
NORM_EPS = 1e-6
Q_BLOCK = 128
N_BRANCH = 4
BRANCH_WIDTH = 256
NEG_BIG = -1e30

MLA_HEADS = 4
MLA_Q_RANK = 256
MLA_KV_RANK = 128
MLA_NOPE = 64
MLA_ROPE = 32
MLA_V = 64

DIFF_HEADS = 4
DIFF_HD = 32
DIFF_VD = 2 * DIFF_HD
DIFF_QK = 2 * DIFF_HEADS * DIFF_HD
DIFF_V_COLS = DIFF_HEADS * DIFF_VD

DN_HEADS = 4
DN_DK = 64
DN_DV = 64
DN_CONV = 5
DN_CHUNK = 64
DN_QKV = DN_HEADS * (2 * DN_DK + DN_DV)

DIL_GROUPS = ((128, 1), (512, 4), (2048, 16))
DIL_HEADS = 4
DIL_HD = 64
DIL_COLS = len(DIL_GROUPS) * DIL_HEADS * DIL_HD

IN_SIZES = (MLA_Q_RANK, MLA_KV_RANK, MLA_ROPE,
            DIFF_QK, DIFF_QK, DIFF_V_COLS,
            DN_QKV, DN_HEADS, DN_HEADS, DN_HEADS, DN_HEADS, DN_HEADS * DN_DV,
            DIL_COLS, DIL_COLS, DIL_COLS)
IN_COLS = sum(IN_SIZES)

kernel_name = 'hybrid_bidir_mla_diff_gdn_dilated_encoder'


def rmsnorm(x, g):
    xf = x.astype(jnp.float32)
    y = xf * lax.rsqrt(jnp.mean(xf * xf, axis=-1, keepdims=True) + NORM_EPS)
    return (y * g.astype(jnp.float32)).astype(x.dtype)


def l2norm(x):
    return x * lax.rsqrt(jnp.sum(x * x, axis=-1, keepdims=True) + NORM_EPS)


def swiglu(x, w1, w3, w2):
    return (jax.nn.silu(x @ w1) * (x @ w3)) @ w2


def rope(x):
    S, d = x.shape[1], x.shape[-1]
    inv = ROPE_THETA ** (-jnp.arange(0, d, 2, dtype=jnp.float32) / d)
    ang = jnp.arange(S, dtype=jnp.float32)[:, None] * inv[None, :]
    cos = jnp.cos(ang)[None, :, None, :]
    sin = jnp.sin(ang)[None, :, None, :]
    xf = x.astype(jnp.float32)
    x1, x2 = xf[..., : d // 2], xf[..., d // 2:]
    return jnp.concatenate([x1 * cos - x2 * sin, x2 * cos + x1 * sin], axis=-1).astype(x.dtype)


def split_cols(t, sizes):
    offs = np.cumsum(sizes)[:-1].tolist()
    return jnp.split(t, offs, axis=-1)


def softmax_attention(q, k, v, scale):
    B, S, H, dk = q.shape
    nb = S // Q_BLOCK
    qb = jnp.moveaxis(q.reshape(B, nb, Q_BLOCK, H, dk), 1, 0)

    def block(qi):
        s = jnp.einsum('bqhd,bkhd->bhqk', qi, k, preferred_element_type=jnp.float32) * scale
        p = jax.nn.softmax(s, axis=-1).astype(v.dtype)
        return jnp.einsum('bhqk,bkhe->bqhe', p, v)

    o = lax.map(block, qb)
    return jnp.moveaxis(o, 0, 1).reshape(B, S, H, v.shape[-1])


def mla_mixer(c_q, c_kv, k_r, q_norm, kv_norm, w_uq, w_ukv):
    B, S, _ = c_q.shape
    q = (rmsnorm(c_q, q_norm) @ w_uq).reshape(B, S, MLA_HEADS, MLA_NOPE + MLA_ROPE)
    q = jnp.concatenate([q[..., :MLA_NOPE], rope(q[..., MLA_NOPE:])], axis=-1)
    kv = (rmsnorm(c_kv, kv_norm) @ w_ukv).reshape(B, S, MLA_HEADS, MLA_NOPE + MLA_V)
    k_nope, v = kv[..., :MLA_NOPE], kv[..., MLA_NOPE:]
    k_rope = rope(k_r[:, :, None, :])
    k = jnp.concatenate([k_nope, jnp.broadcast_to(k_rope, (B, S, MLA_HEADS, MLA_ROPE))], axis=-1)
    o = softmax_attention(q, k, v, (MLA_NOPE + MLA_ROPE) ** -0.5)
    return o.reshape(B, S, MLA_HEADS * MLA_V)


def diff_mixer(q, k, v, lam, subln, lambda_init):
    B, S, _ = q.shape
    H, d = DIFF_HEADS, DIFF_HD
    q = rope(q.reshape(B, S, 2 * H, d)).reshape(B, S, H, 2, d)
    k = rope(k.reshape(B, S, 2 * H, d)).reshape(B, S, H, 2, d)
    v = v.reshape(B, S, H, DIFF_VD)
    lf = lam.astype(jnp.float32)
    lambda_full = jnp.exp(jnp.sum(lf[0] * lf[1])) - jnp.exp(jnp.sum(lf[2] * lf[3])) + lambda_init
    nb = S // Q_BLOCK
    qb = jnp.moveaxis(q.reshape(B, nb, Q_BLOCK, H, 2, d), 1, 0)

    def block(qi):
        s = jnp.einsum('bqhmd,bkhmd->bhmqk', qi, k, preferred_element_type=jnp.float32) * d ** -0.5
        p = jax.nn.softmax(s, axis=-1)
        a = (p[:, :, 0] - lambda_full * p[:, :, 1]).astype(v.dtype)
        return jnp.einsum('bhqk,bkhe->bqhe', a, v)

    o = jnp.moveaxis(lax.map(block, qb), 0, 1).reshape(B, S, H, DIFF_VD)
    o = rmsnorm(o, subln) * (1.0 - lambda_init)
    return o.reshape(B, S, H * DIFF_VD)


def short_conv(x, w):
    K, C = w.shape
    pad = (K - 1) // 2
    return lax.conv_general_dilated(x, w[:, None, :].astype(x.dtype), window_strides=(1,),
                                    padding=[(pad, pad)], dimension_numbers=('NWC', 'WIO', 'NWC'),
                                    feature_group_count=C)


def gated_delta_chunked(q, k, v, beta, g):
    B, S, H, dk = q.shape
    dv = v.shape[-1]
    C = DN_CHUNK
    N = S // C

    def chunks(t):
        return jnp.swapaxes(t.reshape((B, N, C) + t.shape[2:]), 2, 3)

    qc, kc, vc, bc = chunks(q), chunks(k), chunks(v), chunks(beta)
    gc = jnp.cumsum(chunks(g), axis=-1)
    idx = jnp.arange(C)
    incl = idx[:, None] >= idx[None, :]
    strict = idx[:, None] > idx[None, :]
    decay = jnp.exp(jnp.where(incl, gc[..., :, None] - gc[..., None, :], -jnp.inf))
    kb = kc * bc[..., None]
    a_mat = jnp.where(strict, jnp.einsum('bnhid,bnhjd->bnhij', kb, kc) * decay, 0.0)
    t_mat = a_mat + jnp.eye(C, dtype=a_mat.dtype)
    rhs = jnp.concatenate([vc * bc[..., None], kb * jnp.exp(gc)[..., None]], axis=-1)
    sol = lax.linalg.triangular_solve(t_mat, rhs, left_side=True, lower=True, unit_diagonal=True)
    u, w = sol[..., :dv], sol[..., dv:]
    intra = jnp.einsum('bnhid,bnhjd->bnhij', qc, kc) * decay
    q_dec = qc * jnp.exp(gc)[..., None]
    k_dec = kc * jnp.exp(gc[..., -1:] - gc)[..., None]
    g_last = jnp.exp(gc[..., -1])

    def step(state, xs):
        u_i, w_i, q_i, k_i, a_i, gl = xs
        v_new = u_i - jnp.einsum('bhcd,bhde->bhce', w_i, state)
        o_i = jnp.einsum('bhcd,bhde->bhce', q_i, state) + jnp.einsum('bhij,bhje->bhie', a_i, v_new)
        state = state * gl[..., None, None] + jnp.einsum('bhcd,bhce->bhde', k_i, v_new)
        return state, o_i

    xs = tuple(jnp.moveaxis(t, 1, 0) for t in (u, w, q_dec, k_dec, intra, g_last))
    state0 = jnp.zeros((B, H, dk, dv), jnp.float32)
    _, o = lax.scan(step, state0, xs)
    return jnp.transpose(o, (1, 0, 3, 2, 4)).reshape(B, S, H, dv)


def deltanet_mixer(qkv, a_f, b_f, a_b, b_b, z, conv_w, a_log, dt_bias, out_norm):
    B, S, _ = qkv.shape
    H = DN_HEADS
    act = jax.nn.silu(short_conv(qkv, conv_w))
    q, k, v = jnp.split(act, [H * DN_DK, 2 * H * DN_DK], axis=-1)
    q = l2norm(q.reshape(B, S, H, DN_DK).astype(jnp.float32)) * DN_DK ** -0.5
    k = l2norm(k.reshape(B, S, H, DN_DK).astype(jnp.float32))
    v = v.reshape(B, S, H, DN_DV).astype(jnp.float32)
    al = a_log.astype(jnp.float32)
    dtb = dt_bias.astype(jnp.float32)
    beta_f = jax.nn.sigmoid(b_f.astype(jnp.float32))
    g_f = -jnp.exp(al[0]) * jax.nn.softplus(a_f.astype(jnp.float32) + dtb[0])
    beta_b = jax.nn.sigmoid(b_b.astype(jnp.float32))
    g_b = -jnp.exp(al[1]) * jax.nn.softplus(a_b.astype(jnp.float32) + dtb[1])
    flip = lambda t: jnp.flip(t, axis=1)
    o_f = gated_delta_chunked(q, k, v, beta_f, g_f)
    o_b = flip(gated_delta_chunked(flip(q), flip(k), flip(v), flip(beta_b), flip(g_b)))
    o = rmsnorm(o_f + o_b, out_norm) * jax.nn.silu(z.reshape(B, S, H, DN_DV).astype(jnp.float32))
    return o.reshape(B, S, H * DN_DV).astype(qkv.dtype)


def dilated_window_attention(q, k, v, dilation, radius):
    B, S, H, d = q.shape
    L = S // dilation
    blk = radius
    nb = -(-L // blk)
    Lp = nb * blk
    Bp = B * dilation

    def by_residue(t):
        return jnp.transpose(t.reshape(B, L, dilation, H, d), (0, 2, 1, 3, 4)).reshape(Bp, L, H, d)

    qr = jnp.pad(by_residue(q), ((0, 0), (0, Lp - L), (0, 0), (0, 0))).reshape(Bp, nb, blk, H, d)

    def windows(t):
        tb = jnp.pad(by_residue(t), ((0, 0), (blk, Lp - L + blk), (0, 0), (0, 0))).reshape(Bp, nb + 2, blk, H, d)
        return jnp.concatenate([tb[:, :-2], tb[:, 1:-1], tb[:, 2:]], axis=2)

    kw, vw = windows(k), windows(v)
    s = jnp.einsum('bnqhd,bnkhd->bnhqk', qr, kw, preferred_element_type=jnp.float32) * d ** -0.5
    bi = jnp.arange(nb)[:, None, None] * blk
    t_idx = bi + jnp.arange(blk)[None, :, None]
    u_idx = bi - blk + jnp.arange(3 * blk)[None, None, :]
    valid = (jnp.abs(u_idx - t_idx) <= radius) & (u_idx >= 0) & (u_idx < L)
    s = jnp.where(valid[None, :, None], s, NEG_BIG)
    m = jnp.max(s, axis=-1, keepdims=True)
    e = jnp.exp(s - m)
    den = jnp.sum(e, axis=-1, keepdims=True)
    o = jnp.einsum('bnhqk,bnkhd->bnqhd', (e / den).astype(vw.dtype), vw)
    lse = (m + jnp.log(den))[..., 0]
    o = o.reshape(Bp, Lp, H, d)[:, :L]
    o = jnp.transpose(o.reshape(B, dilation, L, H, d), (0, 2, 1, 3, 4)).reshape(B, S, H, d)
    lse = jnp.transpose(lse, (0, 1, 3, 2)).reshape(Bp, Lp, H)[:, :L]
    lse = jnp.transpose(lse.reshape(B, dilation, L, H), (0, 2, 1, 3)).reshape(B, S, H)
    return o, lse


def dilated_mixer(q, k, v):
    B, S, _ = q.shape
    G = len(DIL_GROUPS)
    q = rope(q.reshape(B, S, G * DIL_HEADS, DIL_HD)).reshape(B, S, G, DIL_HEADS, DIL_HD)
    k = rope(k.reshape(B, S, G * DIL_HEADS, DIL_HD)).reshape(B, S, G, DIL_HEADS, DIL_HD)
    v = v.reshape(B, S, G, DIL_HEADS, DIL_HD)
    outs, lses = [], []
    for gi, (window, dilation) in enumerate(DIL_GROUPS):
        o_g, lse_g = dilated_window_attention(q[:, :, gi], k[:, :, gi], v[:, :, gi], dilation,
                                              window // (2 * dilation))
        outs.append(o_g.astype(jnp.float32))
        lses.append(lse_g)
    alpha = jax.nn.softmax(jnp.stack(lses, axis=0), axis=0)
    o = jnp.sum(alpha[..., None] * jnp.stack(outs, axis=0), axis=0)
    return o.reshape(B, S, DIL_HEADS * DIL_HD).astype(q.dtype)


def encoder_layer(x, p_l, w, li):
    x = x + 0.5 * swiglu(rmsnorm(x, w['norm_ff1']), w['ff1_w1'], w['ff1_w3'], w['ff1_w2'])
    h = rmsnorm(x, w['norm_mix'])
    (cq, ckv, kr, bq, bk, bv, cqkv, caf, cbf, cab, cbb, cz, dq, dk, dv) = split_cols(h @ w['w_in'], IN_SIZES)
    y_a = mla_mixer(cq, ckv, kr, w['mla_q_norm'], w['mla_kv_norm'], w['mla_w_uq'], w['mla_w_ukv'])
    y_b = diff_mixer(bq, bk, bv, w['diff_lambda'], w['diff_subln'], 0.8 - 0.6 * math.exp(-0.3 * li))
    y_c = deltanet_mixer(cqkv, caf, cbf, cab, cbb, cz, w['dn_conv'], w['dn_a_log'], w['dn_dt_bias'],
                         w['dn_out_norm'])
    y_d = dilated_mixer(dq, dk, dv)
    merged = None
    for n, y_n in enumerate((y_a, y_b, y_c, y_d)):
        term = jax.nn.sigmoid(h @ w['w_gate'][n]) * (y_n @ w['w_branch'][n])
        merged = term if merged is None else merged + term
    x = x + merged @ w['w_out']
    x = x + 0.5 * swiglu(rmsnorm(x, w['norm_ff2']), w['ff2_w1'], w['ff2_w3'], w['ff2_w2'])
    x = x + jax.nn.sigmoid(rmsnorm(x, w['norm_ple']) @ w['ple_gate']) * (p_l @ w['ple_proj'])
    return x


def run_trunk(x, p, w_stack, norm_final):
    for li in range(DEPTH):
        w = {name: arr[li] for name, arr in w_stack.items()}
        x = encoder_layer(x, p[li], w, li)
    return rmsnorm(x, norm_final)


def setup_inputs(seed: int = 0) -> dict:
    key = jax.random.key(seed)
    ks = iter(jax.random.split(key, 48))
    f32 = jnp.float32
    L = DEPTH

    def nrm(shape, scale):
        return jax.random.normal(next(ks), shape, f32) * scale

    def gain(shape):
        return 1.0 + 0.02 * jax.random.normal(next(ks), shape, f32)

    inp = {}
    inp['x_prompt'] = nrm((BATCH, SEQ, D_MODEL), 1.0)
    inp['x_sample'] = nrm((DEC_BATCH, DEC_SEQ, D_MODEL), 1.0)
    inp['p_prompt'] = nrm((DEPTH, BATCH, SEQ, PLE_DIM), 1.0)
    inp['p_sample'] = nrm((DEPTH, DEC_BATCH, DEC_SEQ, PLE_DIM), 1.0)
    inp['norm_ff1'] = gain((L, D_MODEL))
    inp['ff1_w1'] = nrm((L, D_MODEL, D_FF), D_MODEL ** -0.5)
    inp['ff1_w3'] = nrm((L, D_MODEL, D_FF), D_MODEL ** -0.5)
    inp['ff1_w2'] = nrm((L, D_FF, D_MODEL), D_FF ** -0.5)
    inp['norm_mix'] = gain((L, D_MODEL))
    inp['w_in'] = nrm((L, D_MODEL, IN_COLS), D_MODEL ** -0.5)
    inp['mla_q_norm'] = gain((L, MLA_Q_RANK))
    inp['mla_kv_norm'] = gain((L, MLA_KV_RANK))
    inp['mla_w_uq'] = nrm((L, MLA_Q_RANK, MLA_HEADS * (MLA_NOPE + MLA_ROPE)), MLA_Q_RANK ** -0.5)
    inp['mla_w_ukv'] = nrm((L, MLA_KV_RANK, MLA_HEADS * (MLA_NOPE + MLA_V)), MLA_KV_RANK ** -0.5)
    inp['diff_lambda'] = nrm((L, 4, DIFF_HD), 0.1)
    inp['diff_subln'] = gain((L, DIFF_VD))
    inp['dn_conv'] = nrm((L, DN_CONV, DN_QKV), DN_CONV ** -0.5)
    inp['dn_a_log'] = jnp.log(jax.random.uniform(next(ks), (L, 2, DN_HEADS), f32, 1.0, 16.0))
    dt = jnp.exp(jax.random.uniform(next(ks), (L, 2, DN_HEADS), f32, math.log(1e-3), math.log(1e-1)))
    inp['dn_dt_bias'] = dt + jnp.log(-jnp.expm1(-dt))
    inp['dn_out_norm'] = gain((L, DN_DV))
    inp['w_branch'] = nrm((L, N_BRANCH, BRANCH_WIDTH, D_MODEL), BRANCH_WIDTH ** -0.5)
    inp['w_gate'] = nrm((L, N_BRANCH, D_MODEL, D_MODEL), D_MODEL ** -0.5)
    inp['w_out'] = nrm((L, D_MODEL, D_MODEL), D_MODEL ** -0.5)
    inp['norm_ff2'] = gain((L, D_MODEL))
    inp['ff2_w1'] = nrm((L, D_MODEL, D_FF), D_MODEL ** -0.5)
    inp['ff2_w3'] = nrm((L, D_MODEL, D_FF), D_MODEL ** -0.5)
    inp['ff2_w2'] = nrm((L, D_FF, D_MODEL), D_FF ** -0.5)
    inp['norm_ple'] = gain((L, D_MODEL))
    inp['ple_gate'] = nrm((L, D_MODEL, D_MODEL), D_MODEL ** -0.5)
    inp['ple_proj'] = nrm((L, PLE_DIM, D_MODEL), PLE_DIM ** -0.5)
    inp['norm_final'] = gain((D_MODEL,))
    return inp


def reference(x_prompt, x_sample, p_prompt, p_sample,
              norm_ff1, ff1_w1, ff1_w3, ff1_w2,
              norm_mix, w_in,
              mla_q_norm, mla_kv_norm, mla_w_uq, mla_w_ukv,
              diff_lambda, diff_subln,
              dn_conv, dn_a_log, dn_dt_bias, dn_out_norm,
              w_branch, w_gate, w_out,
              norm_ff2, ff2_w1, ff2_w3, ff2_w2,
              norm_ple, ple_gate, ple_proj,
              norm_final):
    w_stack = {
        'norm_ff1': norm_ff1, 'ff1_w1': ff1_w1, 'ff1_w3': ff1_w3, 'ff1_w2': ff1_w2,
        'norm_mix': norm_mix, 'w_in': w_in,
        'mla_q_norm': mla_q_norm, 'mla_kv_norm': mla_kv_norm, 'mla_w_uq': mla_w_uq, 'mla_w_ukv': mla_w_ukv,
        'diff_lambda': diff_lambda, 'diff_subln': diff_subln,
        'dn_conv': dn_conv, 'dn_a_log': dn_a_log, 'dn_dt_bias': dn_dt_bias, 'dn_out_norm': dn_out_norm,
        'w_branch': w_branch, 'w_gate': w_gate, 'w_out': w_out,
        'norm_ff2': norm_ff2, 'ff2_w1': ff2_w1, 'ff2_w3': ff2_w3, 'ff2_w2': ff2_w2,
        'norm_ple': norm_ple, 'ple_gate': ple_gate, 'ple_proj': ple_proj,
    }
    y_prompt = run_trunk(x_prompt, p_prompt, w_stack, norm_final)
    y_sample = run_trunk(x_sample, p_sample, w_stack, norm_final)
    return (y_prompt, y_sample)
```

```python
import functools
import math

import numpy as np
import jax
import jax.numpy as jnp
from jax import lax
from jax.experimental import pallas as pl
from jax.experimental.pallas import tpu as pltpu

f32 = jnp.float32
bf16 = jnp.bfloat16

D_MODEL = 1024
PLE_DIM = 256
D_FF = 2816
ROPE_THETA = 10000.0
NORM_EPS = 1e-6
NEG_BIG = -1e30

MLA_HEADS = 4
MLA_Q_RANK = 256
MLA_KV_RANK = 128
MLA_NOPE = 64
MLA_ROPE = 32
MLA_V = 64

DIFF_HEADS = 4
DIFF_HD = 32
DIFF_VD = 64

DN_HEADS = 4
DN_DK = 64
DN_DV = 64
DN_CONV = 5
DN_CHUNK = 64
DN_QKV = DN_HEADS * (2 * DN_DK + DN_DV)

DIL_GROUPS = ((128, 1), (512, 4), (2048, 16))
DIL_HEADS = 4
DIL_HD = 64
DIL_RADIUS = 64
DIL_COLS = len(DIL_GROUPS) * DIL_HEADS * DIL_HD

LANES = 128
VMEM_LIMIT = 56 * 1024 * 1024

C_CQKV = 0
C_BQ = 512
C_BK = 768
C_BV = 1024
C_DNQKV = 1536
C_DNG = 2304
C_DNZ = 2432
C_DQ = 2688
C_DK = 3456
C_DV = 4224
C_END = 4992


def _cparams(sem):
    return pltpu.CompilerParams(dimension_semantics=sem, vmem_limit_bytes=VMEM_LIMIT)


def _resident(shape):
    nd = len(shape)
    return pl.BlockSpec(shape, lambda *_: (0,) * nd, pipeline_mode=pl.Buffered(1))


def _dot(a, b):
    return jnp.dot(a, b, preferred_element_type=f32)


def _dot_nt(a, b):
    return lax.dot_general(a, b, (((1,), (1,)), ((), ())), preferred_element_type=f32)


def _rms(x, g):
    ms = jnp.mean(x * x, axis=-1, keepdims=True)
    return x * lax.rsqrt(ms + NORM_EPS) * g


def _silu(x):
    return x * jax.nn.sigmoid(x)


def _split3_dot(x, m):
    x1 = x.astype(bf16)
    r1 = x - x1.astype(f32)
    x2 = r1.astype(bf16)
    x3 = (r1 - x2.astype(f32)).astype(bf16)
    return _dot(x1, m) + _dot(x2, m) + _dot(x3, m)


def _group_sum64(x, bd):
    return _split3_dot(x, bd)


def _rope128(x, c, s1, s2, half):
    return x * c + pltpu.roll(x, LANES - half, 1) * s1 + pltpu.roll(x, half, 1) * s2


def _ffn_kernel(x_ref, g_ref, w1_ref, w3_ref, w2_ref, o_ref, *, fc):
    x = x_ref[...]
    xn = _rms(x, g_ref[...]).astype(bf16)
    y = jnp.zeros_like(x)
    for c in range(D_FF // fc):
        a = _dot(xn, w1_ref[:, c * fc:(c + 1) * fc])
        b = _dot(xn, w3_ref[:, c * fc:(c + 1) * fc])
        y = y + _dot((_silu(a) * b).astype(bf16), w2_ref[c * fc:(c + 1) * fc, :])
    o_ref[...] = x + 0.5 * y


def _ffn(x, g, w1, w3, w2, tm=512, fc=256):
    T = x.shape[0]
    return pl.pallas_call(
        functools.partial(_ffn_kernel, fc=fc),
        grid=(T // tm,),
        in_specs=[pl.BlockSpec((tm, D_MODEL), lambda i: (i, 0)),
                  _resident((1, D_MODEL)),
                  _resident((D_MODEL, D_FF)), _resident((D_MODEL, D_FF)), _resident((D_FF, D_MODEL))],
        out_specs=pl.BlockSpec((tm, D_MODEL), lambda i: (i, 0)),
        out_shape=jax.ShapeDtypeStruct((T, D_MODEL), f32),
        compiler_params=_cparams(("parallel",)),
        name="ffn",
    )(x, g, w1, w3, w2)


def _ple_kernel(x_ref, p_ref, g_ref, wg_ref, wp_ref, gf_ref, o_ref, *, final):
    x = x_ref[...]
    xn = _rms(x, g_ref[...]).astype(bf16)
    gate = jax.nn.sigmoid(_dot(xn, wg_ref[...]))
    y = x + gate * _dot(p_ref[...].astype(bf16), wp_ref[...])
    if final:
        y = _rms(y, gf_ref[...])
    o_ref[...] = y


def _ple(x, p, g, wg, wp, gf, final, tm=512):
    T = x.shape[0]
    return pl.pallas_call(
        functools.partial(_ple_kernel, final=final),
        grid=(T // tm,),
        in_specs=[pl.BlockSpec((tm, D_MODEL), lambda i: (i, 0)),
                  pl.BlockSpec((tm, PLE_DIM), lambda i: (i, 0)),
                  _resident((1, D_MODEL)), _resident((D_MODEL, D_MODEL)), _resident((PLE_DIM, D_MODEL)),
                  _resident((1, D_MODEL))],
        out_specs=pl.BlockSpec((tm, D_MODEL), lambda i: (i, 0)),
        out_shape=jax.ShapeDtypeStruct((T, D_MODEL), f32),
        compiler_params=_cparams(("parallel",)),
        name="ple",
    )(x, p, g, wg, wp, gf)


def _inproj_kernel(x_ref, g_ref, w_ref, qn_ref, kvn_ref, wuq_ref, wuk_ref, wuv_ref, ones_ref, tab_ref,
                   mq_ref, mk_ref, mv_ref, dq_ref, dk_ref, dv_ref, nqkv_ref, ng_ref, nz_ref,
                   lq_ref, lk_ref, lv_ref):
    hb = _rms(x_ref[...], g_ref[...]).astype(bf16)

    def proj(lo, hi):
        return _dot(hb, w_ref[:, lo:hi])

    def tabs(s):
        return tab_ref[3 * s], tab_ref[3 * s + 1], tab_ref[3 * s + 2]

    ones_pat = ones_ref[...]

    c = proj(C_CQKV, C_CQKV + 512)
    cq, ckv, krp = c[:, :MLA_Q_RANK], c[:, MLA_Q_RANK:MLA_Q_RANK + MLA_KV_RANK], c[:, 384:512]
    q = _dot(_rms(cq, qn_ref[...]).astype(bf16), wuq_ref[...])
    kvn = _rms(ckv, kvn_ref[...]).astype(bf16)
    kn = _dot(kvn, wuk_ref[...])
    cb, s1b, s2b = tabs(1)
    q_scale = (MLA_NOPE + MLA_ROPE) ** -0.5
    kr = _rope128(krp, cb, s1b, s2b, MLA_ROPE // 2)
    for h in range(MLA_HEADS):
        sl = slice(h * LANES, (h + 1) * LANES)
        mq_ref[:, sl] = (_rope128(q[:, sl], cb, s1b, s2b, MLA_ROPE // 2) * q_scale).astype(bf16)
        mk_ref[:, sl] = (kn[:, sl] + kr).astype(bf16)
    mv_ref[...] = (_dot(kvn, wuv_ref[...]) + ones_pat).astype(bf16)

    ca, s1a, s2a = tabs(0)
    lane = lax.broadcasted_iota(jnp.int32, (1, LANES), 1)
    bq = proj(C_BQ, C_BQ + 256)
    bk = proj(C_BK, C_BK + 256)
    for ch in range(2):
        sl = slice(ch * LANES, (ch + 1) * LANES)
        qr = _rope128(bq[:, sl], ca, s1a, s2a, DIFF_HD // 2) * (DIFF_HD ** -0.5)
        for slot in range(4):
            hm = ch * 4 + slot
            dq_ref[:, hm * LANES:(hm + 1) * LANES] = jnp.where(lane // DIFF_HD == slot, qr, 0.0).astype(bf16)
        dk_ref[:, sl] = _rope128(bk[:, sl], ca, s1a, s2a, DIFF_HD // 2).astype(bf16)
    dv_ref[...] = (proj(C_BV, C_BV + 512) + ones_pat).astype(bf16)

    nqkv_ref[...] = proj(C_DNQKV, C_DNQKV + DN_QKV)
    ng_ref[...] = proj(C_DNG, C_DNG + LANES)
    nz_ref[...] = proj(C_DNZ, C_DNZ + 256).astype(bf16)

    cc, s1c, s2c = tabs(2)
    lq = proj(C_DQ, C_DQ + DIL_COLS)
    lk = proj(C_DK, C_DK + DIL_COLS)
    for ch in range(DIL_COLS // LANES):
        sl = slice(ch * LANES, (ch + 1) * LANES)
        lq_ref[:, sl] = (_rope128(lq[:, sl], cc, s1c, s2c, DIL_HD // 2) * (DIL_HD ** -0.5)).astype(bf16)
        lk_ref[:, sl] = _rope128(lk[:, sl], cc, s1c, s2c, DIL_HD // 2).astype(bf16)
    lv_ref[...] = proj(C_DV, C_DV + DIL_COLS).astype(bf16)


def _inproj(x, g, w, qn, kvn, wuq, wuk, wuv, ones_pat, tab, n_prompt_tiles, prompt_pos_tiles, sample_pos_tiles,
            tm):
    T = x.shape[0]

    def pos_map(i):
        return (0, jnp.where(i < n_prompt_tiles, i % prompt_pos_tiles, i % sample_pos_tiles), 0)

    def tok(w_, dt):
        return pl.BlockSpec((tm, w_), lambda i: (i, 0)), jax.ShapeDtypeStruct((T, w_), dt)

    outs = [tok(512, bf16), tok(512, bf16), tok(512, bf16),
            tok(1024, bf16), tok(256, bf16), tok(512, bf16),
            tok(DN_QKV, f32), tok(LANES, f32), tok(256, bf16),
            tok(DIL_COLS, bf16), tok(DIL_COLS, bf16), tok(DIL_COLS, bf16)]
    return pl.pallas_call(
        _inproj_kernel,
        grid=(T // tm,),
        in_specs=[pl.BlockSpec((tm, D_MODEL), lambda i: (i, 0)),
                  _resident((1, D_MODEL)), _resident((D_MODEL, C_END)),
                  _resident((1, MLA_Q_RANK)), _resident((1, MLA_KV_RANK)),
                  _resident((MLA_Q_RANK, 512)), _resident((MLA_KV_RANK, 512)), _resident((MLA_KV_RANK, 512)),
                  _resident((1, 512)),
                  pl.BlockSpec((9, tm, LANES), pos_map)],
        out_specs=[o[0] for o in outs],
        out_shape=[o[1] for o in outs],
        compiler_params=_cparams(("parallel",)),
        name="inproj",
    )(x, g, w, qn, kvn, wuq, wuk, wuv, ones_pat, tab)


def _flash_kernel(sc_ref, g_ref, q_ref, k_ref, v_ref, o_ref, m_ref, acc_ref, *, units, diff, nk):
    ki = pl.program_id(2)

    @pl.when(ki == 0)
    def _():
        m_ref[...] = jnp.full(m_ref.shape, NEG_BIG, f32)
        acc_ref[...] = jnp.zeros(acc_ref.shape, f32)

    for u, (qi, kc, vh) in enumerate(units):
        q = q_ref[:, qi * LANES:(qi + 1) * LANES]
        k = k_ref[:, kc * LANES:(kc + 1) * LANES]
        s = _dot_nt(q, k)
        m_prev = m_ref[u]
        m_new = jnp.maximum(m_prev, jnp.max(s, axis=1, keepdims=True))
        alpha = jnp.exp(m_prev - m_new)
        p = jnp.exp(s - m_new[:, :1]).astype(bf16)
        acc_ref[u] = alpha * acc_ref[u] + _dot(p, v_ref[:, vh * LANES:(vh + 1) * LANES])
        m_ref[u] = m_new

    @pl.when(ki == nk - 1)
    def _():
        lane = lax.broadcasted_iota(jnp.int32, (1, LANES), 1)
        low = lane < 64

        def normed(u):
            a = acc_ref[u]
            return a / pltpu.roll(a, 64, 1)

        heads = []
        if diff:
            lam, post = sc_ref[0], sc_ref[1]
            for h in range(DIFF_HEADS):
                o = normed(2 * h) - lam * normed(2 * h + 1)
                ms = jnp.sum(jnp.where(low, o * o, 0.0), axis=1, keepdims=True) * (1.0 / DIFF_VD)
                heads.append(o * lax.rsqrt(ms + NORM_EPS) * g_ref[...] * post)
        else:
            heads = [normed(u) for u in range(len(units))]
        for pr in range(2):
            pair = jnp.where(low, heads[2 * pr], pltpu.roll(heads[2 * pr + 1], 64, 1))
            o_ref[:, pr * LANES:(pr + 1) * LANES] = pair.astype(o_ref.dtype)


def _flash(q, k, v, sc, g, *, B, S, off, units, diff, tq, tk, name):
    nq, nk = S // tq, S // tk
    oq, ok = off // tq, off // tk
    n_acc = len(units)
    return pl.pallas_call(
        functools.partial(_flash_kernel, units=units, diff=diff, nk=nk),
        grid=(B, nq, nk),
        in_specs=[pl.BlockSpec(memory_space=pltpu.SMEM),
                  pl.BlockSpec((1, LANES), lambda b, i, j: (0, 0)),
                  pl.BlockSpec((tq, q.shape[1]), lambda b, i, j: (oq + b * nq + i, 0)),
                  pl.BlockSpec((tk, k.shape[1]), lambda b, i, j: (ok + b * nk + j, 0)),
                  pl.BlockSpec((tk, v.shape[1]), lambda b, i, j: (ok + b * nk + j, 0))],
        out_specs=pl.BlockSpec((tq, 256), lambda b, i, j: (b * nq + i, 0)),
        out_shape=jax.ShapeDtypeStruct((B * S, 256), bf16),
        scratch_shapes=[pltpu.VMEM((n_acc, tq, LANES), f32), pltpu.VMEM((n_acc, tq, LANES), f32)],
        compiler_params=_cparams(("parallel", "parallel", "arbitrary")),
        name=name,
    )(sc, g, q, k, v)


def _dn_prep_kernel(x_ref, xp_ref, xn_ref, w_ref, bd_ref, q_ref, k_ref, v_ref, buf_ref, *,
                    tm, n_prompt_tiles, prompt_seq_tiles, sample_seq_tiles):
    i = pl.program_id(0)
    seq_tiles = jnp.where(i < n_prompt_tiles, prompt_seq_tiles, sample_seq_tiles)
    pos = i % seq_tiles
    buf_ref[0:8, :] = jnp.where(pos == 0, 0.0, xp_ref[...])
    buf_ref[8:8 + tm, :] = x_ref[...]
    buf_ref[8 + tm:16 + tm, :] = jnp.where(pos == seq_tiles - 1, 0.0, xn_ref[...])
    pad = (DN_CONV - 1) // 2
    acc = jnp.zeros((tm, DN_QKV), f32)
    for t in range(DN_CONV):
        acc = acc + buf_ref[pl.ds(8 - pad + t, tm), :] * w_ref[t:t + 1, :]
    act = _silu(acc)
    bd = bd_ref[...]
    hk = DN_HEADS * DN_DK
    q, k = act[:, :hk], act[:, hk:2 * hk]
    q_ref[...] = q * lax.rsqrt(_group_sum64(q * q, bd) + NORM_EPS) * (DN_DK ** -0.5)
    k_ref[...] = k * lax.rsqrt(_group_sum64(k * k, bd) + NORM_EPS)
    v_ref[...] = act[:, 2 * hk:]


def _dn_prep(x, w, bd, n_prompt_tiles, prompt_seq_tiles, sample_seq_tiles, tm):
    T = x.shape[0]
    nb8 = tm // 8
    o = (pl.BlockSpec((tm, 256), lambda i: (i, 0)), jax.ShapeDtypeStruct((T, 256), f32))
    return pl.pallas_call(
        functools.partial(_dn_prep_kernel, tm=tm, n_prompt_tiles=n_prompt_tiles,
                          prompt_seq_tiles=prompt_seq_tiles, sample_seq_tiles=sample_seq_tiles),
        grid=(T // tm,),
        in_specs=[pl.BlockSpec((tm, DN_QKV), lambda i: (i, 0)),
                  pl.BlockSpec((8, DN_QKV), lambda i: (jnp.maximum(i * nb8 - 1, 0), 0)),
                  pl.BlockSpec((8, DN_QKV), lambda i: (jnp.minimum((i + 1) * nb8, T // 8 - 1), 0)),
                  _resident((8, DN_QKV)), _resident((256, 256))],
        out_specs=[o[0]] * 3,
        out_shape=[o[1]] * 3,
        scratch_shapes=[pltpu.VMEM((tm + 16, DN_QKV), f32)],
        compiler_params=_cparams(("parallel",)),
        name="dn_prep",
    )(x, x, x, w, bd)


def _dn_chain(q, k, v, a_row, b_row, eal_row, dtb_row, s_ref, mi, ms, um, bd, bdb):
    n = 4 * DN_CHUNK
    xa = a_row + dtb_row
    g_row = -eal_row * (jnp.maximum(xa, 0.0) + jnp.log1p(jnp.exp(-jnp.abs(xa))))
    beta_row = jax.nn.sigmoid(b_row)
    g8 = jnp.broadcast_to(g_row, (8, n))
    gc_row = _split3_dot(g8, um)[0:1]
    gsum_row = _split3_dot(g8, bdb)[0:1]
    rid = lax.broadcasted_iota(jnp.int32, (n, n), 0)
    stack = jnp.where(rid == 0, gc_row, jnp.where(rid == 1, beta_row, 0.0))
    cols = stack.T
    cg = jnp.broadcast_to(cols[:, 0:1], (n, n))
    cb = jnp.broadcast_to(cols[:, 1:2], (n, n))
    on = bd > 0.0
    decay = jnp.exp(jnp.where(mi > 0.0, cg - gc_row, NEG_BIG))
    eg = jnp.exp(cg)
    ekd = jnp.exp(jnp.where(on, gsum_row - cg, 0.0))

    k4 = jnp.tile(k, (4, 1)) * bd
    q4 = jnp.tile(q, (4, 1)) * bd
    v4 = jnp.tile(v, (4, 1)) * bd
    kb = k4 * cb
    k4b = k4.astype(bf16)
    a_mat = _dot_nt(kb.astype(bf16), k4b) * decay * ms
    intra = _dot_nt(q4.astype(bf16), k4b) * decay
    x = jnp.concatenate([v4 * cb, kb * eg], axis=1)
    b_mat = -a_mat
    for t in range(6):
        bb = b_mat.astype(bf16)
        x = x + _dot(bb, x.astype(bf16))
        if t < 5:
            b_mat = _dot(bb, bb)
    u_mat, w_mat = x[:, :n], x[:, n:]
    s = s_ref[...]
    sb = s.astype(bf16)
    v_new = u_mat - _dot(w_mat.astype(bf16), sb)
    vb = v_new.astype(bf16)
    o_bd = _dot((q4 * eg).astype(bf16), sb) + _dot(intra.astype(bf16), vb)
    o = o_bd[0:64] + o_bd[64:128] + o_bd[128:192] + o_bd[192:256]
    s_ref[...] = s * jnp.exp(gsum_row) + _dot((k4 * ekd).T.astype(bf16), vb)
    return o


def _dn_chunk_kernel(qf_ref, kf_ref, vf_ref, gf_ref, qb_ref, kb_ref, vb_ref, gb_ref, prm_ref,
                     mif_ref, msf_ref, mib_ref, msb_ref, bd_ref, of_ref, ob_ref, sf_ref, sb_ref):
    @pl.when(pl.program_id(1) == 0)
    def _():
        sf_ref[...] = jnp.zeros(sf_ref.shape, f32)
        sb_ref[...] = jnp.zeros(sb_ref.shape, f32)

    bd = bd_ref[...]
    bdb = bd.astype(bf16)
    mif, mib = mif_ref[...], mib_ref[...]
    of_ref[...] = _dn_chain(qf_ref[...], kf_ref[...], vf_ref[...], gf_ref[0:1, :], gf_ref[1:2, :],
                            prm_ref[0:1, :], prm_ref[1:2, :], sf_ref, mif, msf_ref[...],
                            mib.astype(bf16), bd, bdb)
    ob_ref[...] = _dn_chain(qb_ref[...], kb_ref[...], vb_ref[...], gb_ref[2:3, :], gb_ref[3:4, :],
                            prm_ref[2:3, :], prm_ref[3:4, :], sb_ref, mib, msb_ref[...],
                            mif.astype(bf16), bd, bdb)


def _dn_chunk(q, k, v, gates, prm, masks, *, B, S, off):
    C = DN_CHUNK
    N = S // C
    oc = off // C
    fwd = lambda b, c: (oc + b * N + c, 0)
    bwd = lambda b, c: (oc + b * N + N - 1 - c, 0)
    tokf = pl.BlockSpec((C, 256), fwd)
    tokb = pl.BlockSpec((C, 256), bwd)
    gf = pl.BlockSpec((None, 8, 256), lambda b, c: (oc + b * N + c, 0, 0))
    gb = pl.BlockSpec((None, 8, 256), lambda b, c: (oc + b * N + N - 1 - c, 0, 0))
    sq = _resident((256, 256))
    return pl.pallas_call(
        _dn_chunk_kernel,
        grid=(B, N),
        in_specs=[tokf, tokf, tokf, gf, tokb, tokb, tokb, gb, _resident((8, 256)), sq, sq, sq, sq, sq],
        out_specs=[pl.BlockSpec((C, 256), lambda b, c: (b * N + c, 0)),
                   pl.BlockSpec((C, 256), lambda b, c: (b * N + N - 1 - c, 0))],
        out_shape=[jax.ShapeDtypeStruct((B * S, 256), f32)] * 2,
        scratch_shapes=[pltpu.VMEM((256, 256), f32), pltpu.VMEM((256, 256), f32)],
        compiler_params=_cparams(("parallel", "arbitrary")),
        name="dn_chunk",
    )(q, k, v, gates, q, k, v, gates, prm, *masks)


def _dil_kernel(q_ref, km_ref, kp_ref, kn_ref, vm_ref, vp_ref, vn_ref, o_ref, l_ref, *, L, qt):
    j = pl.program_id(2)
    nkeys = qt + 2 * DIL_RADIUS
    q = q_ref[...]
    kf = jnp.concatenate([kp_ref[...], km_ref[...], kn_ref[...]], axis=0)
    vf = jnp.concatenate([vp_ref[...], vm_ref[...], vn_ref[...]], axis=0)
    row = lax.broadcasted_iota(jnp.int32, (qt, nkeys), 0)
    col = lax.broadcasted_iota(jnp.int32, (qt, nkeys), 1)
    u = j * qt - DIL_RADIUS + col
    d = col - DIL_RADIUS - row
    valid = (jnp.abs(d) <= DIL_RADIUS) & (u >= 0) & (u < L)
    lane = lax.broadcasted_iota(jnp.int32, (1, LANES), 1)
    low = lane < DIL_HD
    for pr in range(2):
        sl = slice(pr * LANES, (pr + 1) * LANES)
        qp, kp, vp = q[:, sl], kf[:, sl], vf[:, sl]
        outs, lses = [], []
        for hh in range(2):
            qm = jnp.where(low if hh == 0 else ~low, qp, jnp.zeros_like(qp))
            s = jnp.where(valid, _dot_nt(qm, kp), NEG_BIG)
            m = jnp.max(s, axis=1, keepdims=True)
            e = jnp.exp(s - m)
            den = jnp.sum(e, axis=1, keepdims=True)
            outs.append(_dot((e / den).astype(bf16), vp))
            lses.append(m + jnp.log(den))
        o_ref[:, sl] = jnp.where(low, outs[0], outs[1])
        l_ref[:, sl] = jnp.where(low, lses[0], lses[1])


def _dilated(q, k, v, *, B, S, off, gi, dil, qt=128):
    T = q.shape[0]
    L = S // dil
    nj = L // qt
    r64 = qt // DIL_RADIUS
    qv, kv, vv = (t.reshape(T // dil, dil * DIL_COLS) for t in (q, k, v))
    oq = off // dil // qt
    o64 = off // dil // DIL_RADIUS
    nb = 3

    def main(b, r, j):
        return (oq + b * nj + j, r * nb + gi)

    def prev(b, r, j):
        return (o64 + b * nj * r64 + jnp.maximum(j * r64 - 1, 0), r * nb + gi)

    def nxt(b, r, j):
        return (o64 + b * nj * r64 + jnp.minimum((j + 1) * r64, nj * r64 - 1), r * nb + gi)

    ms = pl.BlockSpec((qt, 256), main)
    ps = pl.BlockSpec((DIL_RADIUS, 256), prev)
    ns = pl.BlockSpec((DIL_RADIUS, 256), nxt)
    osp = pl.BlockSpec((qt, 256), lambda b, r, j: (b * nj + j, r))
    osh = jax.ShapeDtypeStruct((B * L, dil * 256), f32)
    o, l = pl.pallas_call(
        functools.partial(_dil_kernel, L=L, qt=qt),
        grid=(B, dil, nj),
        in_specs=[ms, ms, ps, ns, ms, ps, ns],
        out_specs=[osp, osp],
        out_shape=[osh, osh],
        compiler_params=_cparams(("parallel", "parallel", "parallel")),
        name=f"dilated{gi}",
    )(qv, kv, kv, kv, vv, vv, vv)
    return o.reshape(B * S, 256), l.reshape(B * S, 256)


def _merge_kernel(x_ref, g_ref, ya_ref, yb_ref, of_ref, ob_ref, z_ref, dng_ref, bd_ref,
                  o0_ref, o1_ref, o2_ref, l0_ref, l1_ref, l2_ref, wg_ref, wb_ref, wo_ref, out_ref):
    x = x_ref[...]
    hb = _rms(x, g_ref[...]).astype(bf16)
    o = of_ref[...] + ob_ref[...]
    ms = _group_sum64(o * o, bd_ref[...]) * (1.0 / DN_DV)
    yc = o * lax.rsqrt(ms + NORM_EPS) * dng_ref[...] * _silu(z_ref[...].astype(f32))
    l0, l1, l2 = l0_ref[...], l1_ref[...], l2_ref[...]
    mx = jnp.maximum(jnp.maximum(l0, l1), l2)
    e0, e1, e2 = jnp.exp(l0 - mx), jnp.exp(l1 - mx), jnp.exp(l2 - mx)
    yd = (e0 * o0_ref[...] + e1 * o1_ref[...] + e2 * o2_ref[...]) / (e0 + e1 + e2)
    ys = (ya_ref[...], yb_ref[...], yc.astype(bf16), yd.astype(bf16))
    merged = jnp.zeros_like(x)
    for n in range(4):
        merged = merged + jax.nn.sigmoid(_dot(hb, wg_ref[n])) * _dot(ys[n], wb_ref[n])
    out_ref[...] = x + _dot(merged.astype(bf16), wo_ref[...])


def _merge(x, g, ya, yb, of, ob, z, dng, bd, dil_o, dil_l, wg, wb, wo, tm=512):
    T = x.shape[0]
    t256 = pl.BlockSpec((tm, 256), lambda i: (i, 0))
    return pl.pallas_call(
        _merge_kernel,
        grid=(T // tm,),
        in_specs=[pl.BlockSpec((tm, D_MODEL), lambda i: (i, 0)), _resident((1, D_MODEL)),
                  t256, t256, t256, t256, t256, _resident((1, 256)), _resident((256, 256)),
                  t256, t256, t256, t256, t256, t256,
                  _resident((4, D_MODEL, D_MODEL)), _resident((4, 256, D_MODEL)), _resident((D_MODEL, D_MODEL))],
        out_specs=pl.BlockSpec((tm, D_MODEL), lambda i: (i, 0)),
        out_shape=jax.ShapeDtypeStruct((T, D_MODEL), f32),
        compiler_params=_cparams(("parallel",)),
        name="merge",
    )(x, g, ya, yb, of, ob, z, dng, bd, *dil_o, *dil_l, wg, wb, wo)


def _rope_tables(smax):
    pos = jnp.arange(smax, dtype=f32)[:, None]

    def cs(d):
        inv = ROPE_THETA ** (-jnp.arange(0, d, 2, dtype=f32) / d)
        ang = pos * inv[None, :]
        return jnp.cos(ang), jnp.sin(ang)

    def head_pattern(d):
        c, s = cs(d)
        z = jnp.zeros_like(s)
        return jnp.concatenate([c, c], 1), jnp.concatenate([-s, z], 1), jnp.concatenate([z, s], 1)

    c32, a32, b32 = head_pattern(32)
    c64, a64, b64 = head_pattern(64)
    one = jnp.ones((smax, 64), f32)
    z64 = jnp.zeros((smax, 64), f32)
    z32 = jnp.zeros((smax, 32), f32)
    set_a = [jnp.tile(t, (1, 4)) for t in (c32, a32, b32)]
    set_b = [jnp.concatenate([one, c32, z32], 1), jnp.concatenate([z64, a32, z32], 1),
             jnp.concatenate([z64, b32, z32], 1)]
    set_c = [jnp.tile(t, (1, 2)) for t in (c64, a64, b64)]
    return jnp.stack(set_a + set_b + set_c, 0)


def _head_pad(w, n_heads, width, lo, hi, at=0):
    k = w.shape[0]
    w = w.reshape(k, n_heads, width)[:, :, lo:hi]
    out = jnp.zeros((k, n_heads, LANES), w.dtype).at[:, :, at:at + hi - lo].set(w)
    return out.reshape(k, n_heads * LANES)


def _layer_weights(w, li):
    sizes = (MLA_Q_RANK, MLA_KV_RANK, MLA_ROPE, 256, 256, 256, DN_QKV, 4, 4, 4, 4, 256,
             DIL_COLS, DIL_COLS, DIL_COLS)
    offs = np.cumsum((0,) + sizes)
    win = w['w_in'][li]
    col = lambda n: win[:, offs[n]:offs[n + 1]]
    kr_pad = jnp.zeros((D_MODEL, LANES), f32).at[:, MLA_NOPE:MLA_NOPE + MLA_ROPE].set(col(2))
    gates = jnp.concatenate([col(7), col(8), col(9), col(10), jnp.zeros((D_MODEL, LANES - 16), f32)], 1)
    big = jnp.concatenate([col(0), col(1), kr_pad, col(3), col(4), _head_pad(col(5), 4, 64, 0, 64),
                           col(6), gates, col(11), col(12), col(13), col(14)], 1).astype(bf16)
    assert big.shape[1] == C_END
    ukv = w['mla_w_ukv'][li]
    out = dict(
        w_in=big,
        wuq=_head_pad(w['mla_w_uq'][li], 4, MLA_NOPE + MLA_ROPE, 0, MLA_NOPE + MLA_ROPE).astype(bf16),
        wuk=_head_pad(ukv, 4, MLA_NOPE + MLA_V, 0, MLA_NOPE).astype(bf16),
        wuv=_head_pad(ukv, 4, MLA_NOPE + MLA_V, MLA_NOPE, MLA_NOPE + MLA_V).astype(bf16),
    )
    for n in ('ff1_w1', 'ff1_w3', 'ff1_w2', 'ff2_w1', 'ff2_w3', 'ff2_w2', 'w_gate', 'w_branch', 'w_out',
              'ple_gate', 'ple_proj'):
        out[n] = w[n][li].astype(bf16)
    return out


def _dn_masks():
    idx = np.arange(4 * DN_CHUNK)
    same = (idx[:, None] // DN_CHUNK) == (idx[None, :] // DN_CHUNK)
    i, j = idx[:, None] % DN_CHUNK, idx[None, :] % DN_CHUNK
    mk = lambda m: jnp.asarray((same & m).astype(np.float32))
    return mk(i >= j), mk(i > j), mk(i <= j), mk(i < j), mk(np.ones_like(same))


def _seq_tile(s, pref):
    return min(s, pref)


def kernel(x_prompt, x_sample, p_prompt, p_sample, norm_ff1, ff1_w1, ff1_w3, ff1_w2, norm_mix, w_in, mla_q_norm, mla_kv_norm, mla_w_uq, mla_w_ukv, diff_lambda, diff_subln, dn_conv, dn_a_log, dn_dt_bias, dn_out_norm, w_branch, w_gate, w_out, norm_ff2, ff2_w1, ff2_w3, ff2_w2, norm_ple, ple_gate, ple_proj, norm_final):
    w = dict(norm_ff1=norm_ff1, ff1_w1=ff1_w1, ff1_w3=ff1_w3, ff1_w2=ff1_w2, norm_mix=norm_mix, w_in=w_in,
             mla_q_norm=mla_q_norm, mla_kv_norm=mla_kv_norm, mla_w_uq=mla_w_uq, mla_w_ukv=mla_w_ukv,
             diff_lambda=diff_lambda, diff_subln=diff_subln, dn_conv=dn_conv, dn_a_log=dn_a_log,
             dn_dt_bias=dn_dt_bias, dn_out_norm=dn_out_norm, w_branch=w_branch, w_gate=w_gate, w_out=w_out,
             norm_ff2=norm_ff2, ff2_w1=ff2_w1, ff2_w3=ff2_w3, ff2_w2=ff2_w2, norm_ple=norm_ple,
             ple_gate=ple_gate, ple_proj=ple_proj)
    depth = w_in.shape[0]
    BP, SP, _ = x_prompt.shape
    BS, SS, _ = x_sample.shape
    TP, TS = BP * SP, BS * SS
    T = TP + TS
    groups = ((BP, SP, 0), (BS, SS, TP))
    tm = min(512, SP, SS)
    assert TP % tm == 0 and TS % tm == 0 and SP % tm == 0 and SS % tm == 0

    x = jnp.concatenate([x_prompt.reshape(TP, D_MODEL), x_sample.reshape(TS, D_MODEL)], 0)
    p = jnp.concatenate([p_prompt.reshape(depth, TP, PLE_DIM), p_sample.reshape(depth, TS, PLE_DIM)], 1)

    tab = _rope_tables(max(SP, SS))
    masks = _dn_masks()
    bd_bf = masks[4].astype(bf16)
    ones_pat = jnp.tile(jnp.concatenate([jnp.zeros((1, 64), f32), jnp.ones((1, 64), f32)], 1), (1, 4))
    row = lambda v: v.reshape(1, -1).astype(f32)
    mla_units = tuple((h, h, h) for h in range(MLA_HEADS))
    diff_units = tuple((hm, hm // 4, hm // 2) for hm in range(2 * DIFF_HEADS))
    no_sc = jnp.zeros((2,), f32)
    no_g = jnp.zeros((1, LANES), f32)

    for li in range(depth):
        lw = _layer_weights(w, li)
        x = _ffn(x, row(norm_ff1[li]), lw['ff1_w1'], lw['ff1_w3'], lw['ff1_w2'], tm=tm)

        (mq, mk, mv, dq, dk, dv, nqkv, ng, nz, lq, lk, lv) = _inproj(
            x, row(norm_mix[li]), lw['w_in'], row(mla_q_norm[li]), row(mla_kv_norm[li]),
            lw['wuq'], lw['wuk'], lw['wuv'], ones_pat, tab, TP // tm, SP // tm, SS // tm, tm)

        lf = diff_lambda[li].astype(f32)
        lambda_init = 0.8 - 0.6 * math.exp(-0.3 * li)
        lam = jnp.exp(jnp.sum(lf[0] * lf[1])) - jnp.exp(jnp.sum(lf[2] * lf[3])) + lambda_init
        sc = jnp.stack([lam, jnp.asarray(1.0 - lambda_init, f32)]).astype(f32)
        subln = jnp.tile(row(diff_subln[li]), (1, 2))
        ya, yb = [], []
        for (B, S, off) in groups:
            tq, tk = _seq_tile(S, 512), _seq_tile(S, 512)
            ya.append(_flash(mq, mk, mv, no_sc, no_g, B=B, S=S, off=off, units=mla_units, diff=False,
                             tq=tq, tk=tk, name="mla_attn"))
            yb.append(_flash(dq, dk, dv, sc, subln, B=B, S=S, off=off, units=diff_units, diff=True,
                             tq=tq, tk=tk, name="diff_attn"))
        ya, yb = jnp.concatenate(ya, 0), jnp.concatenate(yb, 0)

        conv_w = jnp.concatenate([dn_conv[li].astype(f32), jnp.zeros((8 - DN_CONV, DN_QKV), f32)], 0)
        nq, nk, nv = _dn_prep(nqkv, conv_w, bd_bf, TP // tm, SP // tm, SS // tm, tm)
        gates = ng[:, :16].reshape(T // DN_CHUNK, DN_CHUNK, 4, DN_HEADS)
        gates = jnp.transpose(gates, (0, 2, 3, 1)).reshape(T // DN_CHUNK, 4, 4 * DN_CHUNK)
        gates = jnp.concatenate([gates, jnp.zeros_like(gates)], 1)
        al, dtb = dn_a_log[li].astype(f32), dn_dt_bias[li].astype(f32)
        rep = lambda v: jnp.repeat(v, DN_CHUNK)[None, :]
        prm = jnp.concatenate([rep(jnp.exp(al[0])), rep(dtb[0]), rep(jnp.exp(al[1])), rep(dtb[1]),
                               jnp.zeros((4, 4 * DN_CHUNK), f32)], 0)
        of, ob = [], []
        for (B, S, off) in groups:
            f_, b_ = _dn_chunk(nq, nk, nv, gates, prm, masks, B=B, S=S, off=off)
            of.append(f_)
            ob.append(b_)
        of, ob = jnp.concatenate(of, 0), jnp.concatenate(ob, 0)

        dil_o, dil_l = [], []
        for gi, (_, dil) in enumerate(DIL_GROUPS):
            os_, ls_ = [], []
            for (B, S, off) in groups:
                o_, l_ = _dilated(lq, lk, lv, B=B, S=S, off=off, gi=gi, dil=dil)
                os_.append(o_)
                ls_.append(l_)
            dil_o.append(jnp.concatenate(os_, 0))
            dil_l.append(jnp.concatenate(ls_, 0))

        x = _merge(x, row(norm_mix[li]), ya, yb, of, ob, nz, jnp.tile(row(dn_out_norm[li]), (1, 4)), bd_bf,
                   dil_o, dil_l, lw['w_gate'], lw['w_branch'], lw['w_out'], tm=tm)
        x = _ffn(x, row(norm_ff2[li]), lw['ff2_w1'], lw['ff2_w3'], lw['ff2_w2'], tm=tm)
        x = _ple(x, p[li], row(norm_ple[li]), lw['ple_gate'], lw['ple_proj'], row(norm_final),
                 final=(li == depth - 1), tm=tm)

    return (x[:TP].reshape(BP, SP, D_MODEL), x[TP:].reshape(BS, SS, D_MODEL))
```

```python
import functools
import math

import numpy as np
import jax
import jax.numpy as jnp
from jax import lax
from jax.experimental import pallas as pl
from jax.experimental.pallas import tpu as pltpu

f32 = jnp.float32
bf16 = jnp.bfloat16

D_MODEL = 1024
PLE_DIM = 256
D_FF = 2816
ROPE_THETA = 10000.0
NORM_EPS = 1e-6
NEG_BIG = -1e30
LOG2E = math.log2(math.e)

MLA_HEADS = 4
MLA_Q_RANK = 256
MLA_KV_RANK = 128
MLA_NOPE = 64
MLA_ROPE = 32
MLA_V = 64

DIFF_HEADS = 4
DIFF_HD = 32
DIFF_VD = 64

DN_HEADS = 4
DN_DK = 64
DN_DV = 64
DN_CONV = 5
DN_CHUNK = 64
DN_QKV = DN_HEADS * (2 * DN_DK + DN_DV)

DIL_GROUPS = ((128, 1), (512, 4), (2048, 16))
DIL_HEADS = 4
DIL_HD = 64
DIL_RADIUS = 64
DIL_COLS = len(DIL_GROUPS) * DIL_HEADS * DIL_HD

LANES = 128
VMEM_LIMIT = 56 * 1024 * 1024

C_CQKV = 0
C_BQ = 512
C_BK = 768
C_BV = 1024
C_DNQKV = 1536
C_DNG = 2304
C_DNZ = 2432
C_DQ = 2688
C_DK = 3456
C_DV = 4224
C_END = 4992


def _cparams(sem):
    return pltpu.CompilerParams(dimension_semantics=sem, vmem_limit_bytes=VMEM_LIMIT)


def _resident(shape):
    nd = len(shape)
    return pl.BlockSpec(shape, lambda *_: (0,) * nd, pipeline_mode=pl.Buffered(1))


def _dot(a, b):
    return jnp.dot(a, b, preferred_element_type=f32)


def _dot_nt(a, b):
    return lax.dot_general(a, b, (((1,), (1,)), ((), ())), preferred_element_type=f32)


def _rms(x, g):
    ms = jnp.mean(x * x, axis=-1, keepdims=True)
    return x * lax.rsqrt(ms + NORM_EPS) * g


def _silu(x):
    return x * jax.nn.sigmoid(x)


def _split3_dot(x, m):
    x1 = x.astype(bf16)
    r1 = x - x1.astype(f32)
    x2 = r1.astype(bf16)
    x3 = (r1 - x2.astype(f32)).astype(bf16)
    return _dot(x1, m) + _dot(x2, m) + _dot(x3, m)


def _group_sum64(x, bd):
    return _split3_dot(x, bd)


def _rope128(x, c, s1, s2, half):
    return x * c + pltpu.roll(x, LANES - half, 1) * s1 + pltpu.roll(x, half, 1) * s2


def _ffn_kernel(x_ref, g_ref, w1_ref, w3_ref, w2_ref, o_ref, *, fc):
    x = x_ref[...]
    xn = _rms(x, g_ref[...]).astype(bf16)
    y = jnp.zeros_like(x)
    for c in range(D_FF // fc):
        a = _dot(xn, w1_ref[:, c * fc:(c + 1) * fc])
        b = _dot(xn, w3_ref[:, c * fc:(c + 1) * fc])
        y = y + _dot((_silu(a) * b).astype(bf16), w2_ref[c * fc:(c + 1) * fc, :])
    o_ref[...] = x + 0.5 * y


def _ffn(x, g, w1, w3, w2, tm=512, fc=256):
    T = x.shape[0]
    return pl.pallas_call(
        functools.partial(_ffn_kernel, fc=fc),
        grid=(T // tm,),
        in_specs=[pl.BlockSpec((tm, D_MODEL), lambda i: (i, 0)),
                  _resident((1, D_MODEL)),
                  _resident((D_MODEL, D_FF)), _resident((D_MODEL, D_FF)), _resident((D_FF, D_MODEL))],
        out_specs=pl.BlockSpec((tm, D_MODEL), lambda i: (i, 0)),
        out_shape=jax.ShapeDtypeStruct((T, D_MODEL), f32),
        compiler_params=_cparams(("parallel",)),
        name="ffn",
    )(x, g, w1, w3, w2)


def _ple_kernel(x_ref, p_ref, g_ref, wg_ref, wp_ref, gf_ref, o_ref, *, final):
    x = x_ref[...]
    xn = _rms(x, g_ref[...]).astype(bf16)
    gate = jax.nn.sigmoid(_dot(xn, wg_ref[...]))
    y = x + gate * _dot(p_ref[...].astype(bf16), wp_ref[...])
    if final:
        y = _rms(y, gf_ref[...])
    o_ref[...] = y


def _ple(x, p, g, wg, wp, gf, final, tm=512):
    T = x.shape[0]
    return pl.pallas_call(
        functools.partial(_ple_kernel, final=final),
        grid=(T // tm,),
        in_specs=[pl.BlockSpec((tm, D_MODEL), lambda i: (i, 0)),
                  pl.BlockSpec((tm, PLE_DIM), lambda i: (i, 0)),
                  _resident((1, D_MODEL)), _resident((D_MODEL, D_MODEL)), _resident((PLE_DIM, D_MODEL)),
                  _resident((1, D_MODEL))],
        out_specs=pl.BlockSpec((tm, D_MODEL), lambda i: (i, 0)),
        out_shape=jax.ShapeDtypeStruct((T, D_MODEL), f32),
        compiler_params=_cparams(("parallel",)),
        name="ple",
    )(x, p, g, wg, wp, gf)


def _inproj_kernel(x_ref, g_ref, w_ref, qn_ref, kvn_ref, wuq_ref, wuk_ref, wuv_ref, ones_ref, tab_ref,
                   mq_ref, mk_ref, mv_ref, dq_ref, dk_ref, dv_ref, nqkv_ref, ng_ref, nz_ref,
                   lq_ref, lk_ref, lv_ref):
    hb = _rms(x_ref[...], g_ref[...]).astype(bf16)

    def proj(lo, hi):
        return _dot(hb, w_ref[:, lo:hi])

    def tabs(s):
        return tab_ref[3 * s], tab_ref[3 * s + 1], tab_ref[3 * s + 2]

    ones_pat = ones_ref[...]

    c = proj(C_CQKV, C_CQKV + 512)
    cq, ckv, krp = c[:, :MLA_Q_RANK], c[:, MLA_Q_RANK:MLA_Q_RANK + MLA_KV_RANK], c[:, 384:512]
    q = _dot(_rms(cq, qn_ref[...]).astype(bf16), wuq_ref[...])
    kvn = _rms(ckv, kvn_ref[...]).astype(bf16)
    kn = _dot(kvn, wuk_ref[...])
    cb, s1b, s2b = tabs(1)
    q_scale = (MLA_NOPE + MLA_ROPE) ** -0.5 * LOG2E
    kr = _rope128(krp, cb, s1b, s2b, MLA_ROPE // 2)
    for h in range(MLA_HEADS):
        sl = slice(h * LANES, (h + 1) * LANES)
        mq_ref[:, sl] = (_rope128(q[:, sl], cb, s1b, s2b, MLA_ROPE // 2) * q_scale).astype(bf16)
        mk_ref[:, sl] = (kn[:, sl] + kr).astype(bf16)
    mv_ref[...] = (_dot(kvn, wuv_ref[...]) + ones_pat).astype(bf16)

    ca, s1a, s2a = tabs(0)
    lane = lax.broadcasted_iota(jnp.int32, (1, LANES), 1)
    bq = proj(C_BQ, C_BQ + 256)
    bk = proj(C_BK, C_BK + 256)
    for ch in range(2):
        sl = slice(ch * LANES, (ch + 1) * LANES)
        qr = _rope128(bq[:, sl], ca, s1a, s2a, DIFF_HD // 2) * (DIFF_HD ** -0.5 * LOG2E)
        for slot in range(4):
            hm = ch * 4 + slot
            dq_ref[:, hm * LANES:(hm + 1) * LANES] = jnp.where(lane // DIFF_HD == slot, qr, 0.0).astype(bf16)
        dk_ref[:, sl] = _rope128(bk[:, sl], ca, s1a, s2a, DIFF_HD // 2).astype(bf16)
    dv_ref[...] = (proj(C_BV, C_BV + 512) + ones_pat).astype(bf16)

    nqkv_ref[...] = proj(C_DNQKV, C_DNQKV + DN_QKV)
    ng_ref[...] = proj(C_DNG, C_DNG + LANES)
    nz_ref[...] = proj(C_DNZ, C_DNZ + 256).astype(bf16)

    cc, s1c, s2c = tabs(2)
    lq = proj(C_DQ, C_DQ + DIL_COLS)
    lk = proj(C_DK, C_DK + DIL_COLS)
    for ch in range(DIL_COLS // LANES):
        sl = slice(ch * LANES, (ch + 1) * LANES)
        lq_ref[:, sl] = (_rope128(lq[:, sl], cc, s1c, s2c, DIL_HD // 2) * (DIL_HD ** -0.5)).astype(bf16)
        lk_ref[:, sl] = _rope128(lk[:, sl], cc, s1c, s2c, DIL_HD // 2).astype(bf16)
    lv_ref[...] = proj(C_DV, C_DV + DIL_COLS).astype(bf16)


def _inproj(x, g, w, qn, kvn, wuq, wuk, wuv, ones_pat, tab, n_prompt_tiles, prompt_pos_tiles, sample_pos_tiles,
            tm):
    T = x.shape[0]

    def pos_map(i):
        return (0, jnp.where(i < n_prompt_tiles, i % prompt_pos_tiles, i % sample_pos_tiles), 0)

    def tok(w_, dt):
        return pl.BlockSpec((tm, w_), lambda i: (i, 0)), jax.ShapeDtypeStruct((T, w_), dt)

    outs = [tok(512, bf16), tok(512, bf16), tok(512, bf16),
            tok(1024, bf16), tok(256, bf16), tok(512, bf16),
            tok(DN_QKV, f32), tok(LANES, f32), tok(256, bf16),
            tok(DIL_COLS, bf16), tok(DIL_COLS, bf16), tok(DIL_COLS, bf16)]
    return pl.pallas_call(
        _inproj_kernel,
        grid=(T // tm,),
        in_specs=[pl.BlockSpec((tm, D_MODEL), lambda i: (i, 0)),
                  _resident((1, D_MODEL)), _resident((D_MODEL, C_END)),
                  _resident((1, MLA_Q_RANK)), _resident((1, MLA_KV_RANK)),
                  _resident((MLA_Q_RANK, 512)), _resident((MLA_KV_RANK, 512)), _resident((MLA_KV_RANK, 512)),
                  _resident((1, 512)),
                  pl.BlockSpec((9, tm, LANES), pos_map)],
        out_specs=[o[0] for o in outs],
        out_shape=[o[1] for o in outs],
        compiler_params=_cparams(("parallel",)),
        name="inproj",
    )(x, g, w, qn, kvn, wuq, wuk, wuv, ones_pat, tab)


def _flash_kernel(sc_ref, g_ref, q_ref, k_ref, v_ref, o_ref, m_ref, acc_ref, *, units, diff, nk):
    ki = pl.program_id(2)

    @pl.when(ki == 0)
    def _():
        m_ref[...] = jnp.full(m_ref.shape, NEG_BIG, f32)
        acc_ref[...] = jnp.zeros(acc_ref.shape, f32)

    nc = k_ref.shape[0] // LANES
    for u, (qi, kc, vh) in enumerate(units):
        q = q_ref[:, qi * LANES:(qi + 1) * LANES]
        k = k_ref[:, kc * LANES:(kc + 1) * LANES]
        s = _dot_nt(q, k)
        m_prev = m_ref[u]
        mx = s[:, 0:LANES]
        for c in range(1, nc):
            mx = jnp.maximum(mx, s[:, c * LANES:(c + 1) * LANES])
        m_new = jnp.maximum(m_prev, jnp.max(mx, axis=1, keepdims=True))
        alpha = jnp.exp2(m_prev - m_new)
        p = jnp.concatenate([jnp.exp2(s[:, c * LANES:(c + 1) * LANES] - m_new).astype(bf16)
                             for c in range(nc)], axis=1)
        acc_ref[u] = alpha * acc_ref[u] + _dot(p, v_ref[:, vh * LANES:(vh + 1) * LANES])
        m_ref[u] = m_new

    @pl.when(ki == nk - 1)
    def _():
        lane = lax.broadcasted_iota(jnp.int32, (1, LANES), 1)
        low = lane < 64

        def normed(u):
            a = acc_ref[u]
            return a / pltpu.roll(a, 64, 1)

        heads = []
        if diff:
            lam, post = sc_ref[0], sc_ref[1]
            for h in range(DIFF_HEADS):
                o = normed(2 * h) - lam * normed(2 * h + 1)
                ms = jnp.sum(jnp.where(low, o * o, 0.0), axis=1, keepdims=True) * (1.0 / DIFF_VD)
                heads.append(o * lax.rsqrt(ms + NORM_EPS) * g_ref[...] * post)
        else:
            heads = [normed(u) for u in range(len(units))]
        for pr in range(2):
            pair = jnp.where(low, heads[2 * pr], pltpu.roll(heads[2 * pr + 1], 64, 1))
            o_ref[:, pr * LANES:(pr + 1) * LANES] = pair.astype(o_ref.dtype)


def _flash(q, k, v, sc, g, *, B, S, off, units, diff, tq, tk, name):
    nq, nk = S // tq, S // tk
    oq, ok = off // tq, off // tk
    n_acc = len(units)
    return pl.pallas_call(
        functools.partial(_flash_kernel, units=units, diff=diff, nk=nk),
        grid=(B, nq, nk),
        in_specs=[pl.BlockSpec(memory_space=pltpu.SMEM),
                  pl.BlockSpec((1, LANES), lambda b, i, j: (0, 0)),
                  pl.BlockSpec((tq, q.shape[1]), lambda b, i, j: (oq + b * nq + i, 0)),
                  pl.BlockSpec((tk, k.shape[1]), lambda b, i, j: (ok + b * nk + j, 0)),
                  pl.BlockSpec((tk, v.shape[1]), lambda b, i, j: (ok + b * nk + j, 0))],
        out_specs=pl.BlockSpec((tq, 256), lambda b, i, j: (b * nq + i, 0)),
        out_shape=jax.ShapeDtypeStruct((B * S, 256), bf16),
        scratch_shapes=[pltpu.VMEM((n_acc, tq, LANES), f32), pltpu.VMEM((n_acc, tq, LANES), f32)],
        compiler_params=_cparams(("parallel", "parallel", "arbitrary")),
        name=name,
    )(sc, g, q, k, v)


def _dn_prep_kernel(x_ref, xp_ref, xn_ref, w_ref, bd_ref, q_ref, k_ref, v_ref, buf_ref, *,
                    tm, n_prompt_tiles, prompt_seq_tiles, sample_seq_tiles):
    i = pl.program_id(0)
    seq_tiles = jnp.where(i < n_prompt_tiles, prompt_seq_tiles, sample_seq_tiles)
    pos = i % seq_tiles
    buf_ref[0:8, :] = jnp.where(pos == 0, 0.0, xp_ref[...])
    buf_ref[8:8 + tm, :] = x_ref[...]
    buf_ref[8 + tm:16 + tm, :] = jnp.where(pos == seq_tiles - 1, 0.0, xn_ref[...])
    pad = (DN_CONV - 1) // 2
    acc = jnp.zeros((tm, DN_QKV), f32)
    for t in range(DN_CONV):
        acc = acc + buf_ref[pl.ds(8 - pad + t, tm), :] * w_ref[t:t + 1, :]
    act = _silu(acc)
    bd = bd_ref[...]
    hk = DN_HEADS * DN_DK
    q, k = act[:, :hk], act[:, hk:2 * hk]
    q_ref[...] = q * lax.rsqrt(_group_sum64(q * q, bd) + NORM_EPS) * (DN_DK ** -0.5)
    k_ref[...] = k * lax.rsqrt(_group_sum64(k * k, bd) + NORM_EPS)
    v_ref[...] = act[:, 2 * hk:]


def _dn_prep(x, w, bd, n_prompt_tiles, prompt_seq_tiles, sample_seq_tiles, tm):
    T = x.shape[0]
    nb8 = tm // 8
    o = (pl.BlockSpec((tm, 256), lambda i: (i, 0)), jax.ShapeDtypeStruct((T, 256), f32))
    return pl.pallas_call(
        functools.partial(_dn_prep_kernel, tm=tm, n_prompt_tiles=n_prompt_tiles,
                          prompt_seq_tiles=prompt_seq_tiles, sample_seq_tiles=sample_seq_tiles),
        grid=(T // tm,),
        in_specs=[pl.BlockSpec((tm, DN_QKV), lambda i: (i, 0)),
                  pl.BlockSpec((8, DN_QKV), lambda i: (jnp.maximum(i * nb8 - 1, 0), 0)),
                  pl.BlockSpec((8, DN_QKV), lambda i: (jnp.minimum((i + 1) * nb8, T // 8 - 1), 0)),
                  _resident((8, DN_QKV)), _resident((256, 256))],
        out_specs=[o[0]] * 3,
        out_shape=[o[1]] * 3,
        scratch_shapes=[pltpu.VMEM((tm + 16, DN_QKV), f32)],
        compiler_params=_cparams(("parallel",)),
        name="dn_prep",
    )(x, x, x, w, bd)


def _dn_chain(q, k, v, a_row, b_row, eal_row, dtb_row, s_ref, mi, ms, um, bd, bdb):
    n = 4 * DN_CHUNK
    xa = a_row + dtb_row
    g_row = -eal_row * (jnp.maximum(xa, 0.0) + jnp.log1p(jnp.exp(-jnp.abs(xa))))
    beta_row = jax.nn.sigmoid(b_row)
    g8 = jnp.broadcast_to(g_row, (8, n))
    gc_row = _split3_dot(g8, um)[0:1]
    gsum_row = _split3_dot(g8, bdb)[0:1]
    rid = lax.broadcasted_iota(jnp.int32, (n, n), 0)
    stack = jnp.where(rid == 0, gc_row, jnp.where(rid == 1, beta_row, 0.0))
    cols = stack.T
    cg = jnp.broadcast_to(cols[:, 0:1], (n, n))
    cb = jnp.broadcast_to(cols[:, 1:2], (n, n))
    on = bd > 0.0
    decay = jnp.exp(jnp.where(mi > 0.0, cg - gc_row, NEG_BIG))
    eg = jnp.exp(cg)
    ekd = jnp.exp(jnp.where(on, gsum_row - cg, 0.0))

    k4 = jnp.tile(k, (4, 1)) * bd
    q4 = jnp.tile(q, (4, 1)) * bd
    v4 = jnp.tile(v, (4, 1)) * bd
    kb = k4 * cb
    k4b = k4.astype(bf16)
    a_mat = _dot_nt(kb.astype(bf16), k4b) * decay * ms
    intra = _dot_nt(q4.astype(bf16), k4b) * decay
    x = jnp.concatenate([v4 * cb, kb * eg], axis=1)
    b_mat = -a_mat
    for t in range(6):
        bb = b_mat.astype(bf16)
        x = x + _dot(bb, x.astype(bf16))
        if t < 5:
            b_mat = _dot(bb, bb)
    u_mat, w_mat = x[:, :n], x[:, n:]
    s = s_ref[...]
    sb = s.astype(bf16)
    v_new = u_mat - _dot(w_mat.astype(bf16), sb)
    vb = v_new.astype(bf16)
    o_bd = _dot((q4 * eg).astype(bf16), sb) + _dot(intra.astype(bf16), vb)
    o = o_bd[0:64] + o_bd[64:128] + o_bd[128:192] + o_bd[192:256]
    s_ref[...] = s * jnp.exp(gsum_row) + _dot((k4 * ekd).T.astype(bf16), vb)
    return o


def _dn_chunk_kernel(qf_ref, kf_ref, vf_ref, gf_ref, qb_ref, kb_ref, vb_ref, gb_ref, prm_ref,
                     mif_ref, msf_ref, mib_ref, msb_ref, bd_ref, of_ref, ob_ref, sf_ref, sb_ref):
    @pl.when(pl.program_id(1) == 0)
    def _():
        sf_ref[...] = jnp.zeros(sf_ref.shape, f32)
        sb_ref[...] = jnp.zeros(sb_ref.shape, f32)

    bd = bd_ref[...]
    bdb = bd.astype(bf16)
    mif, mib = mif_ref[...], mib_ref[...]
    of_ref[...] = _dn_chain(qf_ref[...], kf_ref[...], vf_ref[...], gf_ref[0:1, :], gf_ref[1:2, :],
                            prm_ref[0:1, :], prm_ref[1:2, :], sf_ref, mif, msf_ref[...],
                            mib.astype(bf16), bd, bdb)
    ob_ref[...] = _dn_chain(qb_ref[...], kb_ref[...], vb_ref[...], gb_ref[2:3, :], gb_ref[3:4, :],
                            prm_ref[2:3, :], prm_ref[3:4, :], sb_ref, mib, msb_ref[...],
                            mif.astype(bf16), bd, bdb)


def _dn_chunk(q, k, v, gates, prm, masks, *, B, S, off):
    C = DN_CHUNK
    N = S // C
    oc = off // C
    fwd = lambda b, c: (oc + b * N + c, 0)
    bwd = lambda b, c: (oc + b * N + N - 1 - c, 0)
    tokf = pl.BlockSpec((C, 256), fwd)
    tokb = pl.BlockSpec((C, 256), bwd)
    gf = pl.BlockSpec((None, 8, 256), lambda b, c: (oc + b * N + c, 0, 0))
    gb = pl.BlockSpec((None, 8, 256), lambda b, c: (oc + b * N + N - 1 - c, 0, 0))
    sq = _resident((256, 256))
    return pl.pallas_call(
        _dn_chunk_kernel,
        grid=(B, N),
        in_specs=[tokf, tokf, tokf, gf, tokb, tokb, tokb, gb, _resident((8, 256)), sq, sq, sq, sq, sq],
        out_specs=[pl.BlockSpec((C, 256), lambda b, c: (b * N + c, 0)),
                   pl.BlockSpec((C, 256), lambda b, c: (b * N + N - 1 - c, 0))],
        out_shape=[jax.ShapeDtypeStruct((B * S, 256), f32)] * 2,
        scratch_shapes=[pltpu.VMEM((256, 256), f32), pltpu.VMEM((256, 256), f32)],
        compiler_params=_cparams(("parallel", "arbitrary")),
        name="dn_chunk",
    )(q, k, v, gates, q, k, v, gates, prm, *masks)


def _dil_kernel(q_ref, km_ref, kp_ref, kn_ref, vm_ref, vp_ref, vn_ref, o_ref, l_ref, *, L, qt):
    j = pl.program_id(2)
    nkeys = qt + 2 * DIL_RADIUS
    q = q_ref[...]
    kf = jnp.concatenate([kp_ref[...], km_ref[...], kn_ref[...]], axis=0)
    vf = jnp.concatenate([vp_ref[...], vm_ref[...], vn_ref[...]], axis=0)
    row = lax.broadcasted_iota(jnp.int32, (qt, nkeys), 0)
    col = lax.broadcasted_iota(jnp.int32, (qt, nkeys), 1)
    u = j * qt - DIL_RADIUS + col
    d = col - DIL_RADIUS - row
    valid = (jnp.abs(d) <= DIL_RADIUS) & (u >= 0) & (u < L)
    lane = lax.broadcasted_iota(jnp.int32, (1, LANES), 1)
    low = lane < DIL_HD
    for pr in range(2):
        sl = slice(pr * LANES, (pr + 1) * LANES)
        qp, kp, vp = q[:, sl], kf[:, sl], vf[:, sl]
        outs, lses = [], []
        for hh in range(2):
            qm = jnp.where(low if hh == 0 else ~low, qp, jnp.zeros_like(qp))
            s = jnp.where(valid, _dot_nt(qm, kp), NEG_BIG)
            m = jnp.max(s, axis=1, keepdims=True)
            e = jnp.exp(s - m)
            den = jnp.sum(e, axis=1, keepdims=True)
            outs.append(_dot((e / den).astype(bf16), vp))
            lses.append(m + jnp.log(den))
        o_ref[:, sl] = jnp.where(low, outs[0], outs[1])
        l_ref[:, sl] = jnp.where(low, lses[0], lses[1])


def _dilated(q, k, v, *, B, S, off, gi, dil, qt=128):
    T = q.shape[0]
    L = S // dil
    nj = L // qt
    r64 = qt // DIL_RADIUS
    qv, kv, vv = (t.reshape(T // dil, dil * DIL_COLS) for t in (q, k, v))
    oq = off // dil // qt
    o64 = off // dil // DIL_RADIUS
    nb = 3

    def main(b, r, j):
        return (oq + b * nj + j, r * nb + gi)

    def prev(b, r, j):
        return (o64 + b * nj * r64 + jnp.maximum(j * r64 - 1, 0), r * nb + gi)

    def nxt(b, r, j):
        return (o64 + b * nj * r64 + jnp.minimum((j + 1) * r64, nj * r64 - 1), r * nb + gi)

    ms = pl.BlockSpec((qt, 256), main)
    ps = pl.BlockSpec((DIL_RADIUS, 256), prev)
    ns = pl.BlockSpec((DIL_RADIUS, 256), nxt)
    osp = pl.BlockSpec((qt, 256), lambda b, r, j: (b * nj + j, r))
    osh = jax.ShapeDtypeStruct((B * L, dil * 256), f32)
    o, l = pl.pallas_call(
        functools.partial(_dil_kernel, L=L, qt=qt),
        grid=(B, dil, nj),
        in_specs=[ms, ms, ps, ns, ms, ps, ns],
        out_specs=[osp, osp],
        out_shape=[osh, osh],
        compiler_params=_cparams(("parallel", "parallel", "parallel")),
        name=f"dilated{gi}",
    )(qv, kv, kv, kv, vv, vv, vv)
    return o.reshape(B * S, 256), l.reshape(B * S, 256)


def _merge_kernel(x_ref, g_ref, ya_ref, yb_ref, of_ref, ob_ref, z_ref, dng_ref, bd_ref,
                  o0_ref, o1_ref, o2_ref, l0_ref, l1_ref, l2_ref, wg_ref, wb_ref, wo_ref, out_ref):
    x = x_ref[...]
    hb = _rms(x, g_ref[...]).astype(bf16)
    o = of_ref[...] + ob_ref[...]
    ms = _group_sum64(o * o, bd_ref[...]) * (1.0 / DN_DV)
    yc = o * lax.rsqrt(ms + NORM_EPS) * dng_ref[...] * _silu(z_ref[...].astype(f32))
    l0, l1, l2 = l0_ref[...], l1_ref[...], l2_ref[...]
    mx = jnp.maximum(jnp.maximum(l0, l1), l2)
    e0, e1, e2 = jnp.exp(l0 - mx), jnp.exp(l1 - mx), jnp.exp(l2 - mx)
    yd = (e0 * o0_ref[...] + e1 * o1_ref[...] + e2 * o2_ref[...]) / (e0 + e1 + e2)
    ys = (ya_ref[...], yb_ref[...], yc.astype(bf16), yd.astype(bf16))
    merged = jnp.zeros_like(x)
    for n in range(4):
        merged = merged + jax.nn.sigmoid(_dot(hb, wg_ref[n])) * _dot(ys[n], wb_ref[n])
    out_ref[...] = x + _dot(merged.astype(bf16), wo_ref[...])


def _merge(x, g, ya, yb, of, ob, z, dng, bd, dil_o, dil_l, wg, wb, wo, tm=512):
    T = x.shape[0]
    t256 = pl.BlockSpec((tm, 256), lambda i: (i, 0))
    return pl.pallas_call(
        _merge_kernel,
        grid=(T // tm,),
        in_specs=[pl.BlockSpec((tm, D_MODEL), lambda i: (i, 0)), _resident((1, D_MODEL)),
                  t256, t256, t256, t256, t256, _resident((1, 256)), _resident((256, 256)),
                  t256, t256, t256, t256, t256, t256,
                  _resident((4, D_MODEL, D_MODEL)), _resident((4, 256, D_MODEL)), _resident((D_MODEL, D_MODEL))],
        out_specs=pl.BlockSpec((tm, D_MODEL), lambda i: (i, 0)),
        out_shape=jax.ShapeDtypeStruct((T, D_MODEL), f32),
        compiler_params=_cparams(("parallel",)),
        name="merge",
    )(x, g, ya, yb, of, ob, z, dng, bd, *dil_o, *dil_l, wg, wb, wo)


def _rope_tables(smax):
    pos = jnp.arange(smax, dtype=f32)[:, None]

    def cs(d):
        inv = ROPE_THETA ** (-jnp.arange(0, d, 2, dtype=f32) / d)
        ang = pos * inv[None, :]
        return jnp.cos(ang), jnp.sin(ang)

    def head_pattern(d):
        c, s = cs(d)
        z = jnp.zeros_like(s)
        return jnp.concatenate([c, c], 1), jnp.concatenate([-s, z], 1), jnp.concatenate([z, s], 1)

    c32, a32, b32 = head_pattern(32)
    c64, a64, b64 = head_pattern(64)
    one = jnp.ones((smax, 64), f32)
    z64 = jnp.zeros((smax, 64), f32)
    z32 = jnp.zeros((smax, 32), f32)
    set_a = [jnp.tile(t, (1, 4)) for t in (c32, a32, b32)]
    set_b = [jnp.concatenate([one, c32, z32], 1), jnp.concatenate([z64, a32, z32], 1),
             jnp.concatenate([z64, b32, z32], 1)]
    set_c = [jnp.tile(t, (1, 2)) for t in (c64, a64, b64)]
    return jnp.stack(set_a + set_b + set_c, 0)


def _head_pad(w, n_heads, width, lo, hi, at=0):
    k = w.shape[0]
    w = w.reshape(k, n_heads, width)[:, :, lo:hi]
    out = jnp.zeros((k, n_heads, LANES), w.dtype).at[:, :, at:at + hi - lo].set(w)
    return out.reshape(k, n_heads * LANES)


def _layer_weights(w, li):
    sizes = (MLA_Q_RANK, MLA_KV_RANK, MLA_ROPE, 256, 256, 256, DN_QKV, 4, 4, 4, 4, 256,
             DIL_COLS, DIL_COLS, DIL_COLS)
    offs = np.cumsum((0,) + sizes)
    win = w['w_in'][li]
    col = lambda n: win[:, offs[n]:offs[n + 1]]
    kr_pad = jnp.zeros((D_MODEL, LANES), f32).at[:, MLA_NOPE:MLA_NOPE + MLA_ROPE].set(col(2))
    gates = jnp.concatenate([col(7), col(8), col(9), col(10), jnp.zeros((D_MODEL, LANES - 16), f32)], 1)
    big = jnp.concatenate([col(0), col(1), kr_pad, col(3), col(4), _head_pad(col(5), 4, 64, 0, 64),
                           col(6), gates, col(11), col(12), col(13), col(14)], 1).astype(bf16)
    assert big.shape[1] == C_END
    ukv = w['mla_w_ukv'][li]
    out = dict(
        w_in=big,
        wuq=_head_pad(w['mla_w_uq'][li], 4, MLA_NOPE + MLA_ROPE, 0, MLA_NOPE + MLA_ROPE).astype(bf16),
        wuk=_head_pad(ukv, 4, MLA_NOPE + MLA_V, 0, MLA_NOPE).astype(bf16),
        wuv=_head_pad(ukv, 4, MLA_NOPE + MLA_V, MLA_NOPE, MLA_NOPE + MLA_V).astype(bf16),
    )
    for n in ('ff1_w1', 'ff1_w3', 'ff1_w2', 'ff2_w1', 'ff2_w3', 'ff2_w2', 'w_gate', 'w_branch', 'w_out',
              'ple_gate', 'ple_proj'):
        out[n] = w[n][li].astype(bf16)
    return out


def _dn_masks():
    idx = np.arange(4 * DN_CHUNK)
    same = (idx[:, None] // DN_CHUNK) == (idx[None, :] // DN_CHUNK)
    i, j = idx[:, None] % DN_CHUNK, idx[None, :] % DN_CHUNK
    mk = lambda m: jnp.asarray((same & m).astype(np.float32))
    return mk(i >= j), mk(i > j), mk(i <= j), mk(i < j), mk(np.ones_like(same))


def _seq_tile(s, pref):
    return min(s, pref)


def kernel(x_prompt, x_sample, p_prompt, p_sample, norm_ff1, ff1_w1, ff1_w3, ff1_w2, norm_mix, w_in, mla_q_norm, mla_kv_norm, mla_w_uq, mla_w_ukv, diff_lambda, diff_subln, dn_conv, dn_a_log, dn_dt_bias, dn_out_norm, w_branch, w_gate, w_out, norm_ff2, ff2_w1, ff2_w3, ff2_w2, norm_ple, ple_gate, ple_proj, norm_final):
    w = dict(norm_ff1=norm_ff1, ff1_w1=ff1_w1, ff1_w3=ff1_w3, ff1_w2=ff1_w2, norm_mix=norm_mix, w_in=w_in,
             mla_q_norm=mla_q_norm, mla_kv_norm=mla_kv_norm, mla_w_uq=mla_w_uq, mla_w_ukv=mla_w_ukv,
             diff_lambda=diff_lambda, diff_subln=diff_subln, dn_conv=dn_conv, dn_a_log=dn_a_log,
             dn_dt_bias=dn_dt_bias, dn_out_norm=dn_out_norm, w_branch=w_branch, w_gate=w_gate, w_out=w_out,
             norm_ff2=norm_ff2, ff2_w1=ff2_w1, ff2_w3=ff2_w3, ff2_w2=ff2_w2, norm_ple=norm_ple,
             ple_gate=ple_gate, ple_proj=ple_proj)
    depth = w_in.shape[0]
    BP, SP, _ = x_prompt.shape
    BS, SS, _ = x_sample.shape
    TP, TS = BP * SP, BS * SS
    T = TP + TS
    groups = ((BP, SP, 0), (BS, SS, TP))
    tm = min(512, SP, SS)
    assert TP % tm == 0 and TS % tm == 0 and SP % tm == 0 and SS % tm == 0

    x = jnp.concatenate([x_prompt.reshape(TP, D_MODEL), x_sample.reshape(TS, D_MODEL)], 0)
    p = jnp.concatenate([p_prompt.reshape(depth, TP, PLE_DIM), p_sample.reshape(depth, TS, PLE_DIM)], 1)

    tab = _rope_tables(max(SP, SS))
    masks = _dn_masks()
    bd_bf = masks[4].astype(bf16)
    ones_pat = jnp.tile(jnp.concatenate([jnp.zeros((1, 64), f32), jnp.ones((1, 64), f32)], 1), (1, 4))
    row = lambda v: v.reshape(1, -1).astype(f32)
    mla_units = tuple((h, h, h) for h in range(MLA_HEADS))
    diff_units = tuple((hm, hm // 4, hm // 2) for hm in range(2 * DIFF_HEADS))
    no_sc = jnp.zeros((2,), f32)
    no_g = jnp.zeros((1, LANES), f32)

    for li in range(depth):
        lw = _layer_weights(w, li)
        x = _ffn(x, row(norm_ff1[li]), lw['ff1_w1'], lw['ff1_w3'], lw['ff1_w2'], tm=tm)

        (mq, mk, mv, dq, dk, dv, nqkv, ng, nz, lq, lk, lv) = _inproj(
            x, row(norm_mix[li]), lw['w_in'], row(mla_q_norm[li]), row(mla_kv_norm[li]),
            lw['wuq'], lw['wuk'], lw['wuv'], ones_pat, tab, TP // tm, SP // tm, SS // tm, tm)

        lf = diff_lambda[li].astype(f32)
        lambda_init = 0.8 - 0.6 * math.exp(-0.3 * li)
        lam = jnp.exp(jnp.sum(lf[0] * lf[1])) - jnp.exp(jnp.sum(lf[2] * lf[3])) + lambda_init
        sc = jnp.stack([lam, jnp.asarray(1.0 - lambda_init, f32)]).astype(f32)
        subln = jnp.tile(row(diff_subln[li]), (1, 2))
        ya, yb = [], []
        for (B, S, off) in groups:
            tq = _seq_tile(S, 2048)
            ya.append(_flash(mq, mk, mv, no_sc, no_g, B=B, S=S, off=off, units=mla_units, diff=False,
                             tq=tq, tk=_seq_tile(S, 512), name="mla_attn"))
            yb.append(_flash(dq, dk, dv, sc, subln, B=B, S=S, off=off, units=diff_units, diff=True,
                             tq=tq, tk=_seq_tile(S, 256), name="diff_attn"))
        ya, yb = jnp.concatenate(ya, 0), jnp.concatenate(yb, 0)

        conv_w = jnp.concatenate([dn_conv[li].astype(f32), jnp.zeros((8 - DN_CONV, DN_QKV), f32)], 0)
        nq, nk, nv = _dn_prep(nqkv, conv_w, bd_bf, TP // tm, SP // tm, SS // tm, tm)
        gates = ng[:, :16].reshape(T // DN_CHUNK, DN_CHUNK, 4, DN_HEADS)
        gates = jnp.transpose(gates, (0, 2, 3, 1)).reshape(T // DN_CHUNK, 4, 4 * DN_CHUNK)
        gates = jnp.concatenate([gates, jnp.zeros_like(gates)], 1)
        al, dtb = dn_a_log[li].astype(f32), dn_dt_bias[li].astype(f32)
        rep = lambda v: jnp.repeat(v, DN_CHUNK)[None, :]
        prm = jnp.concatenate([rep(jnp.exp(al[0])), rep(dtb[0]), rep(jnp.exp(al[1])), rep(dtb[1]),
                               jnp.zeros((4, 4 * DN_CHUNK), f32)], 0)
        of, ob = [], []
        for (B, S, off) in groups:
            f_, b_ = _dn_chunk(nq, nk, nv, gates, prm, masks, B=B, S=S, off=off)
            of.append(f_)
            ob.append(b_)
        of, ob = jnp.concatenate(of, 0), jnp.concatenate(ob, 0)

        dil_o, dil_l = [], []
        for gi, (_, dil) in enumerate(DIL_GROUPS):
            os_, ls_ = [], []
            for (B, S, off) in groups:
                o_, l_ = _dilated(lq, lk, lv, B=B, S=S, off=off, gi=gi, dil=dil)
                os_.append(o_)
                ls_.append(l_)
            dil_o.append(jnp.concatenate(os_, 0))
            dil_l.append(jnp.concatenate(ls_, 0))

        x = _merge(x, row(norm_mix[li]), ya, yb, of, ob, nz, jnp.tile(row(dn_out_norm[li]), (1, 4)), bd_bf,
                   dil_o, dil_l, lw['w_gate'], lw['w_branch'], lw['w_out'], tm=tm)
        x = _ffn(x, row(norm_ff2[li]), lw['ff2_w1'], lw['ff2_w3'], lw['ff2_w2'], tm=tm)
        x = _ple(x, p[li], row(norm_ple[li]), lw['ple_gate'], lw['ple_proj'], row(norm_final),
                 final=(li == depth - 1), tm=tm)

    return (x[:TP].reshape(BP, SP, D_MODEL), x[TP:].reshape(BS, SS, D_MODEL))
```

```python
import functools
import math

import numpy as np
import jax
import jax.numpy as jnp
from jax import lax
from jax.experimental import pallas as pl
from jax.experimental.pallas import tpu as pltpu

f32 = jnp.float32
bf16 = jnp.bfloat16

D_MODEL = 1024
PLE_DIM = 256
D_FF = 2816
ROPE_THETA = 10000.0
NORM_EPS = 1e-6
NEG_BIG = -1e30
LOG2E = math.log2(math.e)

MLA_HEADS = 4
MLA_Q_RANK = 256
MLA_KV_RANK = 128
MLA_NOPE = 64
MLA_ROPE = 32
MLA_V = 64

DIFF_HEADS = 4
DIFF_HD = 32
DIFF_VD = 64

DN_HEADS = 4
DN_DK = 64
DN_DV = 64
DN_CONV = 5
DN_CHUNK = 64
DN_QKV = DN_HEADS * (2 * DN_DK + DN_DV)

DIL_GROUPS = ((128, 1), (512, 4), (2048, 16))
DIL_HEADS = 4
DIL_HD = 64
DIL_RADIUS = 64
DIL_COLS = len(DIL_GROUPS) * DIL_HEADS * DIL_HD

LANES = 128
VMEM_LIMIT = 56 * 1024 * 1024

C_CQKV = 0
C_BQ = 512
C_BK = 768
C_BV = 1024
C_DNQKV = 1536
C_DNG = 2304
C_DNZ = 2432
C_DQ = 2688
C_DK = 3456
C_DV = 4224
C_END = 4992


def _cparams(sem):
    return pltpu.CompilerParams(dimension_semantics=sem, vmem_limit_bytes=VMEM_LIMIT)


def _resident(shape):
    nd = len(shape)
    return pl.BlockSpec(shape, lambda *_: (0,) * nd, pipeline_mode=pl.Buffered(1))


def _dot(a, b):
    return jnp.dot(a, b, preferred_element_type=f32)


def _dot_nt(a, b):
    return lax.dot_general(a, b, (((1,), (1,)), ((), ())), preferred_element_type=f32)


def _rms(x, g):
    ms = jnp.mean(x * x, axis=-1, keepdims=True)
    return x * lax.rsqrt(ms + NORM_EPS) * g


def _silu(x):
    return x * jax.nn.sigmoid(x)


def _split3_dot(x, m):
    x1 = x.astype(bf16)
    r1 = x - x1.astype(f32)
    x2 = r1.astype(bf16)
    x3 = (r1 - x2.astype(f32)).astype(bf16)
    return _dot(x1, m) + _dot(x2, m) + _dot(x3, m)


def _group_sum64(x, bd):
    return _split3_dot(x, bd)


def _rope128(x, c, s1, s2, half):
    return x * c + pltpu.roll(x, LANES - half, 1) * s1 + pltpu.roll(x, half, 1) * s2


def _ffn_kernel(x_ref, g_ref, w1_ref, w3_ref, w2_ref, o_ref, *, fc):
    x = x_ref[...]
    xn = _rms(x, g_ref[...]).astype(bf16)
    y = jnp.zeros_like(x)
    for c in range(D_FF // fc):
        a = _dot(xn, w1_ref[:, c * fc:(c + 1) * fc])
        b = _dot(xn, w3_ref[:, c * fc:(c + 1) * fc])
        y = y + _dot((_silu(a) * b).astype(bf16), w2_ref[c * fc:(c + 1) * fc, :])
    o_ref[...] = x + 0.5 * y


def _ffn(x, g, w1, w3, w2, tm=512, fc=256):
    T = x.shape[0]
    return pl.pallas_call(
        functools.partial(_ffn_kernel, fc=fc),
        grid=(T // tm,),
        in_specs=[pl.BlockSpec((tm, D_MODEL), lambda i: (i, 0)),
                  _resident((1, D_MODEL)),
                  _resident((D_MODEL, D_FF)), _resident((D_MODEL, D_FF)), _resident((D_FF, D_MODEL))],
        out_specs=pl.BlockSpec((tm, D_MODEL), lambda i: (i, 0)),
        out_shape=jax.ShapeDtypeStruct((T, D_MODEL), f32),
        compiler_params=_cparams(("parallel",)),
        name="ffn",
    )(x, g, w1, w3, w2)


def _ple_kernel(x_ref, p_ref, g_ref, wg_ref, wp_ref, gf_ref, o_ref, *, final):
    x = x_ref[...]
    xn = _rms(x, g_ref[...]).astype(bf16)
    gate = jax.nn.sigmoid(_dot(xn, wg_ref[...]))
    y = x + gate * _dot(p_ref[...].astype(bf16), wp_ref[...])
    if final:
        y = _rms(y, gf_ref[...])
    o_ref[...] = y


def _ple(x, p, g, wg, wp, gf, final, tm=512):
    T = x.shape[0]
    return pl.pallas_call(
        functools.partial(_ple_kernel, final=final),
        grid=(T // tm,),
        in_specs=[pl.BlockSpec((tm, D_MODEL), lambda i: (i, 0)),
                  pl.BlockSpec((tm, PLE_DIM), lambda i: (i, 0)),
                  _resident((1, D_MODEL)), _resident((D_MODEL, D_MODEL)), _resident((PLE_DIM, D_MODEL)),
                  _resident((1, D_MODEL))],
        out_specs=pl.BlockSpec((tm, D_MODEL), lambda i: (i, 0)),
        out_shape=jax.ShapeDtypeStruct((T, D_MODEL), f32),
        compiler_params=_cparams(("parallel",)),
        name="ple",
    )(x, p, g, wg, wp, gf)


def _inproj_kernel(x_ref, g_ref, w_ref, qn_ref, kvn_ref, wuq_ref, wuk_ref, wuv_ref, ones_ref, tab_ref,
                   mq_ref, mk_ref, mv_ref, dq_ref, dk_ref, dv_ref, nqkv_ref, ng_ref, nz_ref,
                   lq_ref, lk_ref, lv_ref):
    hb = _rms(x_ref[...], g_ref[...]).astype(bf16)

    def proj(lo, hi):
        return _dot(hb, w_ref[:, lo:hi])

    def tabs(s):
        return tab_ref[3 * s], tab_ref[3 * s + 1], tab_ref[3 * s + 2]

    ones_pat = ones_ref[...]

    c = proj(C_CQKV, C_CQKV + 512)
    cq, ckv, krp = c[:, :MLA_Q_RANK], c[:, MLA_Q_RANK:MLA_Q_RANK + MLA_KV_RANK], c[:, 384:512]
    q = _dot(_rms(cq, qn_ref[...]).astype(bf16), wuq_ref[...])
    kvn = _rms(ckv, kvn_ref[...]).astype(bf16)
    kn = _dot(kvn, wuk_ref[...])
    cb, s1b, s2b = tabs(1)
    q_scale = (MLA_NOPE + MLA_ROPE) ** -0.5 * LOG2E
    kr = _rope128(krp, cb, s1b, s2b, MLA_ROPE // 2)
    for h in range(MLA_HEADS):
        sl = slice(h * LANES, (h + 1) * LANES)
        mq_ref[:, sl] = (_rope128(q[:, sl], cb, s1b, s2b, MLA_ROPE // 2) * q_scale).astype(bf16)
        mk_ref[:, sl] = (kn[:, sl] + kr).astype(bf16)
    mv_ref[...] = (_dot(kvn, wuv_ref[...]) + ones_pat).astype(bf16)

    ca, s1a, s2a = tabs(0)
    lane = lax.broadcasted_iota(jnp.int32, (1, LANES), 1)
    bq = proj(C_BQ, C_BQ + 256)
    bk = proj(C_BK, C_BK + 256)
    for ch in range(2):
        sl = slice(ch * LANES, (ch + 1) * LANES)
        qr = _rope128(bq[:, sl], ca, s1a, s2a, DIFF_HD // 2) * (DIFF_HD ** -0.5 * LOG2E)
        for slot in range(4):
            hm = ch * 4 + slot
            dq_ref[:, hm * LANES:(hm + 1) * LANES] = jnp.where(lane // DIFF_HD == slot, qr, 0.0).astype(bf16)
        dk_ref[:, sl] = _rope128(bk[:, sl], ca, s1a, s2a, DIFF_HD // 2).astype(bf16)
    dv_ref[...] = (proj(C_BV, C_BV + 512) + ones_pat).astype(bf16)

    nqkv_ref[...] = proj(C_DNQKV, C_DNQKV + DN_QKV)
    ng_ref[...] = proj(C_DNG, C_DNG + LANES)
    nz_ref[...] = proj(C_DNZ, C_DNZ + 256).astype(bf16)

    cc, s1c, s2c = tabs(2)
    lq = proj(C_DQ, C_DQ + DIL_COLS)
    lk = proj(C_DK, C_DK + DIL_COLS)
    lv = proj(C_DV, C_DV + DIL_COLS)
    for ch in range(DIL_COLS // LANES):
        sl = slice(ch * LANES, (ch + 1) * LANES)
        lq_ref[ch] = _rope128(lq[:, sl], cc, s1c, s2c, DIL_HD // 2) * (DIL_HD ** -0.5)
        lk_ref[ch] = _rope128(lk[:, sl], cc, s1c, s2c, DIL_HD // 2)
        lv_ref[ch] = lv[:, sl]


def _inproj(x, g, w, qn, kvn, wuq, wuk, wuv, ones_pat, tab, n_prompt_tiles, prompt_pos_tiles, sample_pos_tiles,
            tm):
    T = x.shape[0]

    def pos_map(i):
        return (0, jnp.where(i < n_prompt_tiles, i % prompt_pos_tiles, (i - n_prompt_tiles) % sample_pos_tiles), 0)

    def tok(w_, dt):
        return pl.BlockSpec((tm, w_), lambda i: (i, 0)), jax.ShapeDtypeStruct((T, w_), dt)

    dil = (pl.BlockSpec((DIL_COLS // LANES, tm, LANES), lambda i: (0, i, 0)),
           jax.ShapeDtypeStruct((DIL_COLS // LANES, T, LANES), f32))
    outs = [tok(512, bf16), tok(512, bf16), tok(512, bf16),
            tok(1024, bf16), tok(256, bf16), tok(512, bf16),
            tok(DN_QKV, f32), tok(LANES, f32), tok(256, bf16),
            dil, dil, dil]
    return pl.pallas_call(
        _inproj_kernel,
        grid=(T // tm,),
        in_specs=[pl.BlockSpec((tm, D_MODEL), lambda i: (i, 0)),
                  _resident((1, D_MODEL)), _resident((D_MODEL, C_END)),
                  _resident((1, MLA_Q_RANK)), _resident((1, MLA_KV_RANK)),
                  _resident((MLA_Q_RANK, 512)), _resident((MLA_KV_RANK, 512)), _resident((MLA_KV_RANK, 512)),
                  _resident((1, 512)),
                  pl.BlockSpec((9, tm, LANES), pos_map)],
        out_specs=[o[0] for o in outs],
        out_shape=[o[1] for o in outs],
        compiler_params=_cparams(("parallel",)),
        name="inproj",
    )(x, g, w, qn, kvn, wuq, wuk, wuv, ones_pat, tab)


def _flash_kernel(sc_ref, g_ref, q_ref, k_ref, v_ref, o_ref, m_ref, acc_ref, *, units, diff, nk):
    ki = pl.program_id(2)

    @pl.when(ki == 0)
    def _():
        m_ref[...] = jnp.full(m_ref.shape, NEG_BIG, f32)
        acc_ref[...] = jnp.zeros(acc_ref.shape, f32)

    nc = k_ref.shape[0] // LANES
    for u, (qi, kc, vh) in enumerate(units):
        q = q_ref[:, qi * LANES:(qi + 1) * LANES]
        k = k_ref[:, kc * LANES:(kc + 1) * LANES]
        s = _dot_nt(q, k)
        m_prev = m_ref[u]
        mx = s[:, 0:LANES]
        for c in range(1, nc):
            mx = jnp.maximum(mx, s[:, c * LANES:(c + 1) * LANES])
        m_new = jnp.maximum(m_prev, jnp.max(mx, axis=1, keepdims=True))
        alpha = jnp.exp2(m_prev - m_new)
        p = jnp.concatenate([jnp.exp2(s[:, c * LANES:(c + 1) * LANES] - m_new).astype(bf16)
                             for c in range(nc)], axis=1)
        acc_ref[u] = alpha * acc_ref[u] + _dot(p, v_ref[:, vh * LANES:(vh + 1) * LANES])
        m_ref[u] = m_new

    @pl.when(ki == nk - 1)
    def _():
        lane = lax.broadcasted_iota(jnp.int32, (1, LANES), 1)
        low = lane < 64

        def normed(u):
            a = acc_ref[u]
            return a / pltpu.roll(a, 64, 1)

        heads = []
        if diff:
            lam, post = sc_ref[0], sc_ref[1]
            for h in range(DIFF_HEADS):
                o = normed(2 * h) - lam * normed(2 * h + 1)
                ms = jnp.sum(jnp.where(low, o * o, 0.0), axis=1, keepdims=True) * (1.0 / DIFF_VD)
                heads.append(o * lax.rsqrt(ms + NORM_EPS) * g_ref[...] * post)
        else:
            heads = [normed(u) for u in range(len(units))]
        for pr in range(2):
            pair = jnp.where(low, heads[2 * pr], pltpu.roll(heads[2 * pr + 1], 64, 1))
            o_ref[:, pr * LANES:(pr + 1) * LANES] = pair.astype(o_ref.dtype)


def _flash(q, k, v, sc, g, *, B, S, off, units, diff, tq, tk, name):
    nq, nk = S // tq, S // tk
    oq, ok = off // tq, off // tk
    n_acc = len(units)
    return pl.pallas_call(
        functools.partial(_flash_kernel, units=units, diff=diff, nk=nk),
        grid=(B, nq, nk),
        in_specs=[pl.BlockSpec(memory_space=pltpu.SMEM),
                  pl.BlockSpec((1, LANES), lambda b, i, j: (0, 0)),
                  pl.BlockSpec((tq, q.shape[1]), lambda b, i, j: (oq + b * nq + i, 0)),
                  pl.BlockSpec((tk, k.shape[1]), lambda b, i, j: (ok + b * nk + j, 0)),
                  pl.BlockSpec((tk, v.shape[1]), lambda b, i, j: (ok + b * nk + j, 0))],
        out_specs=pl.BlockSpec((tq, 256), lambda b, i, j: (b * nq + i, 0)),
        out_shape=jax.ShapeDtypeStruct((B * S, 256), bf16),
        scratch_shapes=[pltpu.VMEM((n_acc, tq, LANES), f32), pltpu.VMEM((n_acc, tq, LANES), f32)],
        compiler_params=_cparams(("parallel", "parallel", "arbitrary")),
        name=name,
    )(sc, g, q, k, v)


def _dn_prep_kernel(x_ref, xp_ref, xn_ref, w_ref, bd_ref, q_ref, k_ref, v_ref, buf_ref, *,
                    tm, n_prompt_tiles, prompt_seq_tiles, sample_seq_tiles):
    i = pl.program_id(0)
    seq_tiles = jnp.where(i < n_prompt_tiles, prompt_seq_tiles, sample_seq_tiles)
    pos = jnp.where(i < n_prompt_tiles, i, i - n_prompt_tiles) % seq_tiles
    buf_ref[0:8, :] = jnp.where(pos == 0, 0.0, xp_ref[...])
    buf_ref[8:8 + tm, :] = x_ref[...]
    buf_ref[8 + tm:16 + tm, :] = jnp.where(pos == seq_tiles - 1, 0.0, xn_ref[...])
    pad = (DN_CONV - 1) // 2
    acc = jnp.zeros((tm, DN_QKV), f32)
    for t in range(DN_CONV):
        acc = acc + buf_ref[pl.ds(8 - pad + t, tm), :] * w_ref[t:t + 1, :]
    act = _silu(acc)
    bd = bd_ref[...]
    hk = DN_HEADS * DN_DK
    q, k = act[:, :hk], act[:, hk:2 * hk]
    q_ref[...] = q * lax.rsqrt(_group_sum64(q * q, bd) + NORM_EPS) * (DN_DK ** -0.5)
    k_ref[...] = k * lax.rsqrt(_group_sum64(k * k, bd) + NORM_EPS)
    v_ref[...] = act[:, 2 * hk:]


def _dn_prep(x, w, bd, n_prompt_tiles, prompt_seq_tiles, sample_seq_tiles, tm):
    T = x.shape[0]
    nb8 = tm // 8
    o = (pl.BlockSpec((tm, 256), lambda i: (i, 0)), jax.ShapeDtypeStruct((T, 256), f32))
    return pl.pallas_call(
        functools.partial(_dn_prep_kernel, tm=tm, n_prompt_tiles=n_prompt_tiles,
                          prompt_seq_tiles=prompt_seq_tiles, sample_seq_tiles=sample_seq_tiles),
        grid=(T // tm,),
        in_specs=[pl.BlockSpec((tm, DN_QKV), lambda i: (i, 0)),
                  pl.BlockSpec((8, DN_QKV), lambda i: (jnp.maximum(i * nb8 - 1, 0), 0)),
                  pl.BlockSpec((8, DN_QKV), lambda i: (jnp.minimum((i + 1) * nb8, T // 8 - 1), 0)),
                  _resident((8, DN_QKV)), _resident((256, 256))],
        out_specs=[o[0]] * 3,
        out_shape=[o[1]] * 3,
        scratch_shapes=[pltpu.VMEM((tm + 16, DN_QKV), f32)],
        compiler_params=_cparams(("parallel",)),
        name="dn_prep",
    )(x, x, x, w, bd)


def _dn_prep_chunks(chains, bd, bdb):
    n = 4 * DN_CHUNK
    R = range(len(chains))
    rid = lax.broadcasted_iota(jnp.int32, (n, n), 0)
    lane = lax.broadcasted_iota(jnp.int32, (1, LANES), 1)
    low = lane < 64
    on = bd > 0.0

    def fold(x):
        a = x[:, :LANES] + x[:, LANES:]
        return a + pltpu.roll(a, 64, 1)

    g_rows, beta_rows = [], []
    for (q, k, v, a_row, b_row, eal_row, dtb_row, mi, ms, um) in chains:
        xa = a_row + dtb_row
        g_rows.append(-eal_row * (jnp.maximum(xa, 0.0) + jnp.log1p(jnp.exp(-jnp.abs(xa)))))
        beta_rows.append(jax.nn.sigmoid(b_row))
    g8 = [jnp.broadcast_to(g, (8, n)) for g in g_rows]
    gc_rows = [_split3_dot(g8[c], chains[c][9])[0:1] for c in R]
    gsum_rows = [_split3_dot(g8[c], bdb)[0:1] for c in R]
    cols = [jnp.where(rid == 0, gc_rows[c], jnp.where(rid == 1, beta_rows[c], 0.0)).T for c in R]
    cg = [jnp.broadcast_to(cols[c][:, 0:1], (n, n)) for c in R]
    cb = [jnp.broadcast_to(cols[c][:, 1:2], (n, n)) for c in R]
    decay = [jnp.exp(jnp.where(chains[c][7] > 0.0, cg[c] - gc_rows[c], NEG_BIG)) for c in R]
    eg = [jnp.exp(cg[c]) for c in R]
    k4 = [jnp.tile(chains[c][1], (4, 1)) * bd for c in R]
    kb = [k4[c] * cb[c] for c in R]
    k4b = [k4[c].astype(bf16) for c in R]
    a_mat = [_dot_nt(kb[c].astype(bf16), k4b[c]) * decay[c] * chains[c][8] for c in R]
    x = [jnp.where(low, fold(jnp.tile(chains[c][2], (4, 1)) * bd * cb[c]), fold(kb[c] * eg[c])) for c in R]
    b_mat = [-a for a in a_mat]
    for t in range(6):
        bb = [b.astype(bf16) for b in b_mat]
        x = [x[c] + _dot(bb[c], x[c].astype(bf16)) for c in R]
        if t < 5:
            b_mat = [_dot(bb[c], bb[c]) for c in R]
    q4 = [jnp.tile(chains[c][0], (4, 1)) * bd for c in R]
    intra = [(_dot_nt(q4[c].astype(bf16), k4b[c]) * decay[c]).astype(bf16) for c in R]
    out = []
    for c in R:
        xr = pltpu.roll(x[c], 64, 1)
        u_bd = jnp.tile(jnp.where(low, x[c], xr), (1, 2)) * bd
        w_bd = (jnp.tile(jnp.where(low, xr, x[c]), (1, 2)) * bd).astype(bf16)
        ekd = jnp.exp(jnp.where(on, gsum_rows[c] - cg[c], 0.0))
        out.append((u_bd, w_bd, (q4[c] * eg[c]).astype(bf16), intra[c], (k4[c] * ekd).T.astype(bf16),
                    jnp.exp(gsum_rows[c])))
    return out


def _dn_step(s, pre):
    u_bd, w_bd, qg, intra, kd_t, egl = pre
    sb = s.astype(bf16)
    v_new = u_bd - _dot(w_bd, sb)
    vb = v_new.astype(bf16)
    o_bd = _dot(qg, sb) + _dot(intra, vb)
    o = o_bd[0:64] + o_bd[64:128] + o_bd[128:192] + o_bd[192:256]
    return s * egl + _dot(kd_t, vb), o


def _dn_chunk_kernel(qf_ref, kf_ref, vf_ref, gf_ref, qb_ref, kb_ref, vb_ref, gb_ref, prm_ref,
                     mif_ref, msf_ref, mib_ref, msb_ref, bd_ref, of_ref, ob_ref, sf_ref, sb_ref, *, nch):
    @pl.when(pl.program_id(1) == 0)
    def _():
        sf_ref[...] = jnp.zeros(sf_ref.shape, f32)
        sb_ref[...] = jnp.zeros(sb_ref.shape, f32)

    C = DN_CHUNK
    bd = bd_ref[...]
    bdb = bd.astype(bf16)
    mif, mib = mif_ref[...], mib_ref[...]
    umf, umb = mib.astype(bf16), mif.astype(bf16)
    msf, msb = msf_ref[...], msb_ref[...]
    chains = []
    for c in range(nch):
        r = slice(c * C, (c + 1) * C)
        chains.append((qf_ref[r, :], kf_ref[r, :], vf_ref[r, :], gf_ref[c, 0:1, :], gf_ref[c, 1:2, :],
                       prm_ref[0:1, :], prm_ref[1:2, :], mif, msf, umf))
    for c in range(nch):
        r = slice(c * C, (c + 1) * C)
        chains.append((qb_ref[r, :], kb_ref[r, :], vb_ref[r, :], gb_ref[c, 2:3, :], gb_ref[c, 3:4, :],
                       prm_ref[2:3, :], prm_ref[3:4, :], mib, msb, umb))
    pre = _dn_prep_chunks(chains, bd, bdb)
    pf, pb = pre[:nch], pre[nch:]
    sf, sb = sf_ref[...], sb_ref[...]
    for c in range(nch):
        sf, o = _dn_step(sf, pf[c])
        of_ref[c * C:(c + 1) * C, :] = o
        cr = nch - 1 - c
        sb, o = _dn_step(sb, pb[cr])
        ob_ref[cr * C:(cr + 1) * C, :] = o
    sf_ref[...] = sf
    sb_ref[...] = sb


def _dn_chunk(q, k, v, gates, prm, masks, *, B, S, off, nch=4):
    R = DN_CHUNK * nch
    N = S // R
    oc = off // R
    fwd = lambda b, c: (oc + b * N + c, 0)
    bwd = lambda b, c: (oc + b * N + N - 1 - c, 0)
    tokf = pl.BlockSpec((R, 256), fwd)
    tokb = pl.BlockSpec((R, 256), bwd)
    gf = pl.BlockSpec((nch, 8, 256), lambda b, c: (oc + b * N + c, 0, 0))
    gb = pl.BlockSpec((nch, 8, 256), lambda b, c: (oc + b * N + N - 1 - c, 0, 0))
    sq = _resident((256, 256))
    return pl.pallas_call(
        functools.partial(_dn_chunk_kernel, nch=nch),
        grid=(B, N),
        in_specs=[tokf, tokf, tokf, gf, tokb, tokb, tokb, gb, _resident((8, 256)), sq, sq, sq, sq, sq],
        out_specs=[pl.BlockSpec((R, 256), lambda b, c: (b * N + c, 0)),
                   pl.BlockSpec((R, 256), lambda b, c: (b * N + N - 1 - c, 0))],
        out_shape=[jax.ShapeDtypeStruct((B * S, 256), f32)] * 2,
        scratch_shapes=[pltpu.VMEM((256, 256), f32), pltpu.VMEM((256, 256), f32)],
        compiler_params=_cparams(("parallel", "arbitrary")),
        name="dn_chunk",
    )(q, k, v, gates, q, k, v, gates, prm, *masks)


def _dil_kernel(q_ref, k_ref, v_ref, kp0, kn0, kp1, kn1, kp2, kn2, vp0, vn0, vp1, vn1, vp2, vn2,
                o_ref, os_ref, ls_ref, *, ts, nb, n_prompt_tiles, prompt_seq_tiles, sample_seq_tiles):
    i = pl.program_id(0)
    in_prompt = i < n_prompt_tiles
    seq_tiles = jnp.where(in_prompt, prompt_seq_tiles, sample_seq_tiles)
    pos = jnp.where(in_prompt, i, i - n_prompt_tiles) % seq_tiles
    first, last = pos == 0, pos == seq_tiles - 1
    halos = ((kp0, kn0, vp0, vn0), (kp1, kn1, vp1, vn1), (kp2, kn2, vp2, vn2))
    lane = lax.broadcasted_iota(jnp.int32, (1, LANES), 1)
    low = lane < DIL_HD
    R = DIL_RADIUS
    items = []
    for gi, (_, d) in enumerate(DIL_GROUPS):
        n = ts // d
        qb = min(LANES, n)
        for r in range(d):
            for j in range(n // qb):
                for pr in range(2):
                    items.append((gi, d, n, qb, r, j, pr))

    def load_keys(main, prv, nxt, gi, d, n, qb, r, t0, pr):
        parts = []
        lo, hi = t0 - R, t0 + qb + R
        if lo < 0:
            parts.append(prv[pr, pl.ds(r + d * (lo + R), -lo, stride=d), :])
        a, b = max(lo, 0), min(hi, n)
        parts.append(main[2 * gi + pr, pl.ds(r + d * a, b - a, stride=d), :])
        if hi > n:
            parts.append(nxt[pr, pl.ds(r, hi - n, stride=d), :])
        return jnp.concatenate(parts, axis=0).astype(bf16)

    masks = {}

    def mask_for(qb, n, t0):
        key = (qb, n, t0)
        if key not in masks:
            nkeys = qb + 2 * R
            row = lax.broadcasted_iota(jnp.int32, (qb, nkeys), 0)
            col = lax.broadcasted_iota(jnp.int32, (qb, nkeys), 1)
            valid = jnp.abs(col - R - row) <= R
            if t0 - R < 0:
                valid = valid & ((col >= R - t0) | jnp.logical_not(first))
            if t0 + qb + R > n:
                valid = valid & ((col < n + R - t0) | jnp.logical_not(last))
            masks[key] = valid
        return masks[key]

    for b0 in range(0, len(items), nb):
        batch = items[b0:b0 + nb]
        qs, ks, vs, vl = [], [], [], []
        for (gi, d, n, qb, r, j, pr) in batch:
            t0 = j * qb
            kp, kn, vp, vn = halos[gi]
            qs.append(q_ref[2 * gi + pr, pl.ds(r + d * t0, qb, stride=d), :].astype(bf16))
            ks.append(load_keys(k_ref, kp, kn, gi, d, n, qb, r, t0, pr))
            vs.append(load_keys(v_ref, vp, vn, gi, d, n, qb, r, t0, pr))
            vl.append(mask_for(qb, n, t0))
        U = [(i_, hh) for i_ in range(len(batch)) for hh in range(2)]
        qm = [jnp.where(low if hh == 0 else ~low, qs[i_], jnp.zeros_like(qs[i_])) for (i_, hh) in U]
        sc = [jnp.where(vl[i_], _dot_nt(qm[u], ks[i_]), NEG_BIG) for u, (i_, hh) in enumerate(U)]
        mm = [jnp.max(x, axis=1, keepdims=True) for x in sc]
        ee = [jnp.exp(sc[u] - mm[u]) for u in range(len(U))]
        dd = [jnp.sum(x, axis=1, keepdims=True) for x in ee]
        pv = [_dot((ee[u] / dd[u]).astype(bf16), vs[i_]) for u, (i_, hh) in enumerate(U)]
        ll = [mm[u] + jnp.log(dd[u]) for u in range(len(U))]
        for i_, (gi, d, n, qb, r, j, pr) in enumerate(batch):
            dst = pl.ds(r + d * j * qb, qb, stride=d)
            os_ref[2 * gi + pr, dst, :] = jnp.where(low, pv[2 * i_], pv[2 * i_ + 1])
            ls_ref[2 * gi + pr, dst, :] = jnp.where(low, ll[2 * i_], ll[2 * i_ + 1])
    for pr in range(2):
        l0, l1, l2 = ls_ref[pr], ls_ref[2 + pr], ls_ref[4 + pr]
        mx = jnp.maximum(jnp.maximum(l0, l1), l2)
        e0, e1, e2 = jnp.exp(l0 - mx), jnp.exp(l1 - mx), jnp.exp(l2 - mx)
        y = (e0 * os_ref[pr] + e1 * os_ref[2 + pr] + e2 * os_ref[4 + pr]) / (e0 + e1 + e2)
        o_ref[:, pr * LANES:(pr + 1) * LANES] = y.astype(o_ref.dtype)


def _dilated(q, k, v, n_prompt_tiles, prompt_seq_tiles, sample_seq_tiles, ts, nb=8):
    T = q.shape[1]
    main = pl.BlockSpec((6, ts, LANES), lambda i: (0, i, 0))
    halo_specs = []
    for gi, (_, d) in enumerate(DIL_GROUPS):
        h = DIL_RADIUS * d
        per = ts // h
        halo_specs.append(pl.BlockSpec((2, h, LANES),
                                       lambda i, per=per, gi=gi: (gi, jnp.maximum(i * per - 1, 0), 0)))
        halo_specs.append(pl.BlockSpec((2, h, LANES),
                                       lambda i, per=per, gi=gi, h=h: (gi, jnp.minimum((i + 1) * per, T // h - 1), 0)))
    return pl.pallas_call(
        functools.partial(_dil_kernel, ts=ts, nb=nb, n_prompt_tiles=n_prompt_tiles,
                          prompt_seq_tiles=prompt_seq_tiles, sample_seq_tiles=sample_seq_tiles),
        grid=(T // ts,),
        in_specs=[main, main, main] + halo_specs + halo_specs,
        out_specs=pl.BlockSpec((ts, 256), lambda i: (i, 0)),
        out_shape=jax.ShapeDtypeStruct((T, 256), bf16),
        scratch_shapes=[pltpu.VMEM((6, ts, LANES), f32), pltpu.VMEM((6, ts, LANES), f32)],
        compiler_params=_cparams(("parallel",)),
        name="dilated",
    )(q, k, v, *([k] * 6), *([v] * 6))


def _merge_kernel(x_ref, g_ref, ya_ref, yb_ref, of_ref, ob_ref, z_ref, dng_ref, bd_ref, yd_ref,
                  wg_ref, wb_ref, wo_ref, out_ref):
    x = x_ref[...]
    hb = _rms(x, g_ref[...]).astype(bf16)
    o = of_ref[...] + ob_ref[...]
    ms = _group_sum64(o * o, bd_ref[...]) * (1.0 / DN_DV)
    yc = o * lax.rsqrt(ms + NORM_EPS) * dng_ref[...] * _silu(z_ref[...].astype(f32))
    ys = (ya_ref[...], yb_ref[...], yc.astype(bf16), yd_ref[...])
    merged = jnp.zeros_like(x)
    for n in range(4):
        merged = merged + jax.nn.sigmoid(_dot(hb, wg_ref[n])) * _dot(ys[n], wb_ref[n])
    out_ref[...] = x + _dot(merged.astype(bf16), wo_ref[...])


def _merge(x, g, ya, yb, of, ob, z, dng, bd, yd, wg, wb, wo, tm=512):
    T = x.shape[0]
    t256 = pl.BlockSpec((tm, 256), lambda i: (i, 0))
    return pl.pallas_call(
        _merge_kernel,
        grid=(T // tm,),
        in_specs=[pl.BlockSpec((tm, D_MODEL), lambda i: (i, 0)), _resident((1, D_MODEL)),
                  t256, t256, t256, t256, t256, _resident((1, 256)), _resident((256, 256)), t256,
                  _resident((4, D_MODEL, D_MODEL)), _resident((4, 256, D_MODEL)), _resident((D_MODEL, D_MODEL))],
        out_specs=pl.BlockSpec((tm, D_MODEL), lambda i: (i, 0)),
        out_shape=jax.ShapeDtypeStruct((T, D_MODEL), f32),
        compiler_params=_cparams(("parallel",)),
        name="merge",
    )(x, g, ya, yb, of, ob, z, dng, bd, yd, wg, wb, wo)


def _rope_tables(smax):
    pos = jnp.arange(smax, dtype=f32)[:, None]

    def cs(d):
        inv = ROPE_THETA ** (-jnp.arange(0, d, 2, dtype=f32) / d)
        ang = pos * inv[None, :]
        return jnp.cos(ang), jnp.sin(ang)

    def head_pattern(d):
        c, s = cs(d)
        z = jnp.zeros_like(s)
        return jnp.concatenate([c, c], 1), jnp.concatenate([-s, z], 1), jnp.concatenate([z, s], 1)

    c32, a32, b32 = head_pattern(32)
    c64, a64, b64 = head_pattern(64)
    one = jnp.ones((smax, 64), f32)
    z64 = jnp.zeros((smax, 64), f32)
    z32 = jnp.zeros((smax, 32), f32)
    set_a = [jnp.tile(t, (1, 4)) for t in (c32, a32, b32)]
    set_b = [jnp.concatenate([one, c32, z32], 1), jnp.concatenate([z64, a32, z32], 1),
             jnp.concatenate([z64, b32, z32], 1)]
    set_c = [jnp.tile(t, (1, 2)) for t in (c64, a64, b64)]
    return jnp.stack(set_a + set_b + set_c, 0)


def _head_pad(w, n_heads, width, lo, hi, at=0):
    k = w.shape[0]
    w = w.reshape(k, n_heads, width)[:, :, lo:hi]
    out = jnp.zeros((k, n_heads, LANES), w.dtype).at[:, :, at:at + hi - lo].set(w)
    return out.reshape(k, n_heads * LANES)


def _layer_weights(w, li):
    sizes = (MLA_Q_RANK, MLA_KV_RANK, MLA_ROPE, 256, 256, 256, DN_QKV, 4, 4, 4, 4, 256,
             DIL_COLS, DIL_COLS, DIL_COLS)
    offs = np.cumsum((0,) + sizes)
    win = w['w_in'][li]
    col = lambda n: win[:, offs[n]:offs[n + 1]]
    kr_pad = jnp.zeros((D_MODEL, LANES), f32).at[:, MLA_NOPE:MLA_NOPE + MLA_ROPE].set(col(2))
    gates = jnp.concatenate([col(7), col(8), col(9), col(10), jnp.zeros((D_MODEL, LANES - 16), f32)], 1)
    big = jnp.concatenate([col(0), col(1), kr_pad, col(3), col(4), _head_pad(col(5), 4, 64, 0, 64),
                           col(6), gates, col(11), col(12), col(13), col(14)], 1).astype(bf16)
    assert big.shape[1] == C_END
    ukv = w['mla_w_ukv'][li]
    out = dict(
        w_in=big,
        wuq=_head_pad(w['mla_w_uq'][li], 4, MLA_NOPE + MLA_ROPE, 0, MLA_NOPE + MLA_ROPE).astype(bf16),
        wuk=_head_pad(ukv, 4, MLA_NOPE + MLA_V, 0, MLA_NOPE).astype(bf16),
        wuv=_head_pad(ukv, 4, MLA_NOPE + MLA_V, MLA_NOPE, MLA_NOPE + MLA_V).astype(bf16),
    )
    for n in ('ff1_w1', 'ff1_w3', 'ff1_w2', 'ff2_w1', 'ff2_w3', 'ff2_w2', 'w_gate', 'w_branch', 'w_out',
              'ple_gate', 'ple_proj'):
        out[n] = w[n][li].astype(bf16)
    return out


def _dn_masks():
    idx = np.arange(4 * DN_CHUNK)
    same = (idx[:, None] // DN_CHUNK) == (idx[None, :] // DN_CHUNK)
    i, j = idx[:, None] % DN_CHUNK, idx[None, :] % DN_CHUNK
    mk = lambda m: jnp.asarray((same & m).astype(np.float32))
    return mk(i >= j), mk(i > j), mk(i <= j), mk(i < j), mk(np.ones_like(same))


def _seq_tile(s, pref):
    return min(s, pref)


def kernel(x_prompt, x_sample, p_prompt, p_sample, norm_ff1, ff1_w1, ff1_w3, ff1_w2, norm_mix, w_in, mla_q_norm, mla_kv_norm, mla_w_uq, mla_w_ukv, diff_lambda, diff_subln, dn_conv, dn_a_log, dn_dt_bias, dn_out_norm, w_branch, w_gate, w_out, norm_ff2, ff2_w1, ff2_w3, ff2_w2, norm_ple, ple_gate, ple_proj, norm_final):
    w = dict(norm_ff1=norm_ff1, ff1_w1=ff1_w1, ff1_w3=ff1_w3, ff1_w2=ff1_w2, norm_mix=norm_mix, w_in=w_in,
             mla_q_norm=mla_q_norm, mla_kv_norm=mla_kv_norm, mla_w_uq=mla_w_uq, mla_w_ukv=mla_w_ukv,
             diff_lambda=diff_lambda, diff_subln=diff_subln, dn_conv=dn_conv, dn_a_log=dn_a_log,
             dn_dt_bias=dn_dt_bias, dn_out_norm=dn_out_norm, w_branch=w_branch, w_gate=w_gate, w_out=w_out,
             norm_ff2=norm_ff2, ff2_w1=ff2_w1, ff2_w3=ff2_w3, ff2_w2=ff2_w2, norm_ple=norm_ple,
             ple_gate=ple_gate, ple_proj=ple_proj)
    depth = w_in.shape[0]
    BP, SP, _ = x_prompt.shape
    BS, SS, _ = x_sample.shape
    TP, TS = BP * SP, BS * SS
    T = TP + TS
    groups = ((BP, SP, 0), (BS, SS, TP))
    tm = min(512, SP, SS)
    assert TP % tm == 0 and TS % tm == 0 and SP % tm == 0 and SS % tm == 0

    x = jnp.concatenate([x_prompt.reshape(TP, D_MODEL), x_sample.reshape(TS, D_MODEL)], 0)
    p = jnp.concatenate([p_prompt.reshape(depth, TP, PLE_DIM), p_sample.reshape(depth, TS, PLE_DIM)], 1)

    tab = _rope_tables(max(SP, SS))
    masks = _dn_masks()
    bd_bf = masks[4].astype(bf16)
    ones_pat = jnp.tile(jnp.concatenate([jnp.zeros((1, 64), f32), jnp.ones((1, 64), f32)], 1), (1, 4))
    row = lambda v: v.reshape(1, -1).astype(f32)
    mla_units = tuple((h, h, h) for h in range(MLA_HEADS))
    diff_units = tuple((hm, hm // 4, hm // 2) for hm in range(2 * DIFF_HEADS))
    no_sc = jnp.zeros((2,), f32)
    no_g = jnp.zeros((1, LANES), f32)

    for li in range(depth):
        lw = _layer_weights(w, li)
        x = _ffn(x, row(norm_ff1[li]), lw['ff1_w1'], lw['ff1_w3'], lw['ff1_w2'], tm=tm)

        (mq, mk, mv, dq, dk, dv, nqkv, ng, nz, lq, lk, lv) = _inproj(
            x, row(norm_mix[li]), lw['w_in'], row(mla_q_norm[li]), row(mla_kv_norm[li]),
            lw['wuq'], lw['wuk'], lw['wuv'], ones_pat, tab, TP // tm, SP // tm, SS // tm, tm)

        lf = diff_lambda[li].astype(f32)
        lambda_init = 0.8 - 0.6 * math.exp(-0.3 * li)
        lam = jnp.exp(jnp.sum(lf[0] * lf[1])) - jnp.exp(jnp.sum(lf[2] * lf[3])) + lambda_init
        sc = jnp.stack([lam, jnp.asarray(1.0 - lambda_init, f32)]).astype(f32)
        subln = jnp.tile(row(diff_subln[li]), (1, 2))
        ya, yb = [], []
        for (B, S, off) in groups:
            tq = _seq_tile(S, 2048)
            ya.append(_flash(mq, mk, mv, no_sc, no_g, B=B, S=S, off=off, units=mla_units, diff=False,
                             tq=tq, tk=_seq_tile(S, 512), name="mla_attn"))
            yb.append(_flash(dq, dk, dv, sc, subln, B=B, S=S, off=off, units=diff_units, diff=True,
                             tq=tq, tk=_seq_tile(S, 256), name="diff_attn"))
        ya, yb = jnp.concatenate(ya, 0), jnp.concatenate(yb, 0)

        conv_w = jnp.concatenate([dn_conv[li].astype(f32), jnp.zeros((8 - DN_CONV, DN_QKV), f32)], 0)
        nq, nk, nv = _dn_prep(nqkv, conv_w, bd_bf, TP // tm, SP // tm, SS // tm, tm)
        gates = ng[:, :16].reshape(T // DN_CHUNK, DN_CHUNK, 4, DN_HEADS)
        gates = jnp.transpose(gates, (0, 2, 3, 1)).reshape(T // DN_CHUNK, 4, 4 * DN_CHUNK)
        gates = jnp.concatenate([gates, jnp.zeros_like(gates)], 1)
        al, dtb = dn_a_log[li].astype(f32), dn_dt_bias[li].astype(f32)
        rep = lambda v: jnp.repeat(v, DN_CHUNK)[None, :]
        prm = jnp.concatenate([rep(jnp.exp(al[0])), rep(dtb[0]), rep(jnp.exp(al[1])), rep(dtb[1]),
                               jnp.zeros((4, 4 * DN_CHUNK), f32)], 0)
        of, ob = [], []
        for (B, S, off) in groups:
            f_, b_ = _dn_chunk(nq, nk, nv, gates, prm, masks, B=B, S=S, off=off)
            of.append(f_)
            ob.append(b_)
        of, ob = jnp.concatenate(of, 0), jnp.concatenate(ob, 0)

        ts = min(1024, SP, SS)
        yd = _dilated(lq, lk, lv, TP // ts, SP // ts, SS // ts, ts)

        x = _merge(x, row(norm_mix[li]), ya, yb, of, ob, nz, jnp.tile(row(dn_out_norm[li]), (1, 4)), bd_bf,
                   yd, lw['w_gate'], lw['w_branch'], lw['w_out'], tm=tm)
        x = _ffn(x, row(norm_ff2[li]), lw['ff2_w1'], lw['ff2_w3'], lw['ff2_w2'], tm=tm)
        x = _ple(x, p[li], row(norm_ple[li]), lw['ple_gate'], lw['ple_proj'], row(norm_final),
                 final=(li == depth - 1), tm=tm)

    return (x[:TP].reshape(BP, SP, D_MODEL), x[TP:].reshape(BS, SS, D_MODEL))
```

```python
import functools
import math

import numpy as np
import jax
import jax.numpy as jnp
from jax import lax
from jax.experimental import pallas as pl
from jax.experimental.pallas import tpu as pltpu

f32 = jnp.float32
bf16 = jnp.bfloat16

D_MODEL = 1024
PLE_DIM = 256
D_FF = 2816
ROPE_THETA = 10000.0
NORM_EPS = 1e-6
NEG_BIG = -1e30
LOG2E = math.log2(math.e)

MLA_HEADS = 4
MLA_Q_RANK = 256
MLA_KV_RANK = 128
MLA_NOPE = 64
MLA_ROPE = 32
MLA_V = 64

DIFF_HEADS = 4
DIFF_HD = 32
DIFF_VD = 64

DN_HEADS = 4
DN_DK = 64
DN_DV = 64
DN_CONV = 5
DN_CHUNK = 64
DN_QKV = DN_HEADS * (2 * DN_DK + DN_DV)

DIL_GROUPS = ((128, 1), (512, 4), (2048, 16))
DIL_HEADS = 4
DIL_HD = 64
DIL_RADIUS = 64
DIL_COLS = len(DIL_GROUPS) * DIL_HEADS * DIL_HD

LANES = 128
VMEM_LIMIT = 56 * 1024 * 1024

C_CQKV = 0
C_BQ = 512
C_BK = 768
C_BV = 1024
C_DNQKV = 1536
C_DNG = 2304
C_DNZ = 2432
C_DQ = 2688
C_DK = 3456
C_DV = 4224
C_END = 4992


def _cparams(sem):
    return pltpu.CompilerParams(dimension_semantics=sem, vmem_limit_bytes=VMEM_LIMIT)


def _resident(shape):
    nd = len(shape)
    return pl.BlockSpec(shape, lambda *_: (0,) * nd, pipeline_mode=pl.Buffered(1))


def _dot(a, b):
    return jnp.dot(a, b, preferred_element_type=f32)


def _dot_nt(a, b):
    return lax.dot_general(a, b, (((1,), (1,)), ((), ())), preferred_element_type=f32)


def _rms(x, g):
    ms = jnp.mean(x * x, axis=-1, keepdims=True)
    return x * lax.rsqrt(ms + NORM_EPS) * g


def _silu(x):
    return x * jax.nn.sigmoid(x)


def _split3_dot(x, m):
    x1 = x.astype(bf16)
    r1 = x - x1.astype(f32)
    x2 = r1.astype(bf16)
    x3 = (r1 - x2.astype(f32)).astype(bf16)
    return _dot(x1, m) + _dot(x2, m) + _dot(x3, m)


def _group_sum64(x, bd):
    return _split3_dot(x, bd)


def _rope128(x, c, s1, s2, half):
    return x * c + pltpu.roll(x, LANES - half, 1) * s1 + pltpu.roll(x, half, 1) * s2


def _ffn_kernel(*refs, fc, n_in, ple, final, n_out, n_prompt_tiles):
    refs = list(refs)
    take = lambda k: [refs.pop(0) for _ in range(k)]
    x_refs = take(n_in)
    g_ref, w1_ref, w3_ref, w2_ref = take(4)
    if ple:
        p_refs = take(2)
        gp_ref, wg_ref, wp_ref, gf_ref = take(4)
    o_refs = take(n_out)
    in_prompt = pl.program_id(0) < n_prompt_tiles

    def pick(rs):
        return rs[0][...] if len(rs) == 1 else jnp.where(in_prompt, rs[0][...], rs[1][...])

    x = pick(x_refs)
    xn = _rms(x, g_ref[...]).astype(bf16)
    y = jnp.zeros_like(x)
    for c in range(D_FF // fc):
        a = _dot(xn, w1_ref[:, c * fc:(c + 1) * fc])
        b = _dot(xn, w3_ref[:, c * fc:(c + 1) * fc])
        y = y + _dot((_silu(a) * b).astype(bf16), w2_ref[c * fc:(c + 1) * fc, :])
    y = x + 0.5 * y
    if ple:
        yn = _rms(y, gp_ref[...]).astype(bf16)
        gate = jax.nn.sigmoid(_dot(yn, wg_ref[...]))
        y = y + gate * _dot(pick(p_refs).astype(bf16), wp_ref[...])
        if final:
            y = _rms(y, gf_ref[...])
    if n_out == 1:
        o_refs[0][...] = y
    else:
        @pl.when(in_prompt)
        def _():
            o_refs[0][...] = y

        @pl.when(jnp.logical_not(in_prompt))
        def _():
            o_refs[1][...] = y


def _ffn(xs, g, w1, w3, w2, *, n_prompt_tiles, tm, ple=None, final=False, split_out=False, fc=256):
    n_in = len(xs)
    tp = n_prompt_tiles * tm
    T = sum(x.shape[0] for x in xs)

    def tok_specs(width, dual, lead=None):
        def mk(fn):
            if lead is None:
                return pl.BlockSpec((tm, width), lambda i: (fn(i), 0))
            return pl.BlockSpec((None, tm, width), lambda i: (lead, fn(i), 0))
        if not dual:
            return [mk(lambda i: i)]
        return [mk(lambda i: jnp.minimum(i, n_prompt_tiles - 1)), mk(lambda i: jnp.maximum(i - n_prompt_tiles, 0))]

    args = list(xs) + [g, w1, w3, w2]
    in_specs = tok_specs(D_MODEL, n_in == 2) + [_resident((1, D_MODEL)), _resident((D_MODEL, D_FF)),
                                                _resident((D_MODEL, D_FF)), _resident((D_FF, D_MODEL))]
    if ple is not None:
        ps, li, gp, wg, wp, gf = ple
        args += list(ps) + [gp, wg, wp, gf]
        in_specs += tok_specs(PLE_DIM, True, lead=li) + [_resident((1, D_MODEL)), _resident((D_MODEL, D_MODEL)),
                                                         _resident((PLE_DIM, D_MODEL)), _resident((1, D_MODEL))]
    if split_out:
        out_specs = [pl.BlockSpec((tm, D_MODEL), lambda i: (jnp.minimum(i, n_prompt_tiles - 1), 0)),
                     pl.BlockSpec((tm, D_MODEL), lambda i: (jnp.maximum(i - n_prompt_tiles, 0), 0))]
        out_shape = [jax.ShapeDtypeStruct((tp, D_MODEL), f32), jax.ShapeDtypeStruct((T - tp, D_MODEL), f32)]
    else:
        out_specs = pl.BlockSpec((tm, D_MODEL), lambda i: (i, 0))
        out_shape = jax.ShapeDtypeStruct((T, D_MODEL), f32)
    return pl.pallas_call(
        functools.partial(_ffn_kernel, fc=fc, n_in=n_in, ple=ple is not None, final=final,
                          n_out=2 if split_out else 1, n_prompt_tiles=n_prompt_tiles),
        grid=(T // tm,),
        in_specs=in_specs,
        out_specs=out_specs,
        out_shape=out_shape,
        compiler_params=_cparams(("arbitrary",) if split_out else ("parallel",)),
        name="ffn_ple" if ple is not None else "ffn",
    )(*args)


def _inproj_kernel(x_ref, g_ref, w_ref, qn_ref, kvn_ref, wuq_ref, wuk_ref, wuv_ref, ones_ref, tab_ref,
                   mq_ref, mk_ref, mv_ref, dq_ref, dk_ref, dv_ref, nqkv_ref, ng_ref, nz_ref,
                   lq_ref, lk_ref, lv_ref):
    hb = _rms(x_ref[...], g_ref[...]).astype(bf16)

    def proj(lo, hi):
        return _dot(hb, w_ref[:, lo:hi])

    def tabs(s):
        return tab_ref[3 * s], tab_ref[3 * s + 1], tab_ref[3 * s + 2]

    ones_pat = ones_ref[...]

    c = proj(C_CQKV, C_CQKV + 512)
    cq, ckv, krp = c[:, :MLA_Q_RANK], c[:, MLA_Q_RANK:MLA_Q_RANK + MLA_KV_RANK], c[:, 384:512]
    q = _dot(_rms(cq, qn_ref[...]).astype(bf16), wuq_ref[...])
    kvn = _rms(ckv, kvn_ref[...]).astype(bf16)
    kn = _dot(kvn, wuk_ref[...])
    cb, s1b, s2b = tabs(1)
    q_scale = (MLA_NOPE + MLA_ROPE) ** -0.5 * LOG2E
    kr = _rope128(krp, cb, s1b, s2b, MLA_ROPE // 2)
    for h in range(MLA_HEADS):
        sl = slice(h * LANES, (h + 1) * LANES)
        mq_ref[:, sl] = (_rope128(q[:, sl], cb, s1b, s2b, MLA_ROPE // 2) * q_scale).astype(bf16)
        mk_ref[:, sl] = (kn[:, sl] + kr).astype(bf16)
    mv_ref[...] = (_dot(kvn, wuv_ref[...]) + ones_pat).astype(bf16)

    ca, s1a, s2a = tabs(0)
    lane = lax.broadcasted_iota(jnp.int32, (1, LANES), 1)
    bq = proj(C_BQ, C_BQ + 256)
    bk = proj(C_BK, C_BK + 256)
    for ch in range(2):
        sl = slice(ch * LANES, (ch + 1) * LANES)
        qr = _rope128(bq[:, sl], ca, s1a, s2a, DIFF_HD // 2) * (DIFF_HD ** -0.5 * LOG2E)
        for slot in range(4):
            hm = ch * 4 + slot
            dq_ref[:, hm * LANES:(hm + 1) * LANES] = jnp.where(lane // DIFF_HD == slot, qr, 0.0).astype(bf16)
        dk_ref[:, sl] = _rope128(bk[:, sl], ca, s1a, s2a, DIFF_HD // 2).astype(bf16)
    dv_ref[...] = (proj(C_BV, C_BV + 512) + ones_pat).astype(bf16)

    nqkv_ref[...] = proj(C_DNQKV, C_DNQKV + DN_QKV)
    ng_ref[...] = proj(C_DNG, C_DNG + LANES)
    nz_ref[...] = proj(C_DNZ, C_DNZ + 256).astype(bf16)

    cc, s1c, s2c = tabs(2)
    lq = proj(C_DQ, C_DQ + DIL_COLS)
    lk = proj(C_DK, C_DK + DIL_COLS)
    lv = proj(C_DV, C_DV + DIL_COLS)
    for ch in range(DIL_COLS // LANES):
        sl = slice(ch * LANES, (ch + 1) * LANES)
        lq_ref[ch] = _rope128(lq[:, sl], cc, s1c, s2c, DIL_HD // 2) * (DIL_HD ** -0.5)
        lk_ref[ch] = _rope128(lk[:, sl], cc, s1c, s2c, DIL_HD // 2)
        lv_ref[ch] = lv[:, sl]


def _inproj(x, g, w, qn, kvn, wuq, wuk, wuv, ones_pat, tab, n_prompt_tiles, prompt_pos_tiles, sample_pos_tiles,
            tm):
    T = x.shape[0]

    def pos_map(i):
        return (0, jnp.where(i < n_prompt_tiles, i % prompt_pos_tiles, (i - n_prompt_tiles) % sample_pos_tiles), 0)

    def tok(w_, dt):
        return pl.BlockSpec((tm, w_), lambda i: (i, 0)), jax.ShapeDtypeStruct((T, w_), dt)

    dil = (pl.BlockSpec((DIL_COLS // LANES, tm, LANES), lambda i: (0, i, 0)),
           jax.ShapeDtypeStruct((DIL_COLS // LANES, T, LANES), f32))
    outs = [tok(512, bf16), tok(512, bf16), tok(512, bf16),
            tok(1024, bf16), tok(256, bf16), tok(512, bf16),
            tok(DN_QKV, f32), tok(LANES, f32), tok(256, bf16),
            dil, dil, dil]
    return pl.pallas_call(
        _inproj_kernel,
        grid=(T // tm,),
        in_specs=[pl.BlockSpec((tm, D_MODEL), lambda i: (i, 0)),
                  _resident((1, D_MODEL)), _resident((D_MODEL, C_END)),
                  _resident((1, MLA_Q_RANK)), _resident((1, MLA_KV_RANK)),
                  _resident((MLA_Q_RANK, 512)), _resident((MLA_KV_RANK, 512)), _resident((MLA_KV_RANK, 512)),
                  _resident((1, 512)),
                  pl.BlockSpec((9, tm, LANES), pos_map)],
        out_specs=[o[0] for o in outs],
        out_shape=[o[1] for o in outs],
        compiler_params=_cparams(("parallel",)),
        name="inproj",
    )(x, g, w, qn, kvn, wuq, wuk, wuv, ones_pat, tab)


def _flash_kernel(sc_ref, g_ref, q_ref, k_ref, v_ref, o_ref, m_ref, acc_ref, *, units, diff, nk):
    ki = pl.program_id(2)

    @pl.when(ki == 0)
    def _():
        m_ref[...] = jnp.full(m_ref.shape, NEG_BIG, f32)
        acc_ref[...] = jnp.zeros(acc_ref.shape, f32)

    nc = k_ref.shape[0] // LANES
    for u, (qi, kc, vh) in enumerate(units):
        q = q_ref[:, qi * LANES:(qi + 1) * LANES]
        k = k_ref[:, kc * LANES:(kc + 1) * LANES]
        s = _dot_nt(q, k)
        m_prev = m_ref[u]
        mx = s[:, 0:LANES]
        for c in range(1, nc):
            mx = jnp.maximum(mx, s[:, c * LANES:(c + 1) * LANES])
        m_new = jnp.maximum(m_prev, jnp.max(mx, axis=1, keepdims=True))
        alpha = jnp.exp2(m_prev - m_new)
        p = jnp.concatenate([jnp.exp2(s[:, c * LANES:(c + 1) * LANES] - m_new).astype(bf16)
                             for c in range(nc)], axis=1)
        acc_ref[u] = alpha * acc_ref[u] + _dot(p, v_ref[:, vh * LANES:(vh + 1) * LANES])
        m_ref[u] = m_new

    @pl.when(ki == nk - 1)
    def _():
        lane = lax.broadcasted_iota(jnp.int32, (1, LANES), 1)
        low = lane < 64

        def normed(u):
            a = acc_ref[u]
            return a / pltpu.roll(a, 64, 1)

        heads = []
        if diff:
            lam, post = sc_ref[0], sc_ref[1]
            for h in range(DIFF_HEADS):
                o = normed(2 * h) - lam * normed(2 * h + 1)
                ms = jnp.sum(jnp.where(low, o * o, 0.0), axis=1, keepdims=True) * (1.0 / DIFF_VD)
                heads.append(o * lax.rsqrt(ms + NORM_EPS) * g_ref[...] * post)
        else:
            heads = [normed(u) for u in range(len(units))]
        for pr in range(2):
            pair = jnp.where(low, heads[2 * pr], pltpu.roll(heads[2 * pr + 1], 64, 1))
            o_ref[:, pr * LANES:(pr + 1) * LANES] = pair.astype(o_ref.dtype)


def _flash(q, k, v, sc, g, *, B, S, off, units, diff, tq, tk, name):
    nq, nk = S // tq, S // tk
    oq, ok = off // tq, off // tk
    n_acc = len(units)
    return pl.pallas_call(
        functools.partial(_flash_kernel, units=units, diff=diff, nk=nk),
        grid=(B, nq, nk),
        in_specs=[pl.BlockSpec(memory_space=pltpu.SMEM),
                  pl.BlockSpec((1, LANES), lambda b, i, j: (0, 0)),
                  pl.BlockSpec((tq, q.shape[1]), lambda b, i, j: (oq + b * nq + i, 0)),
                  pl.BlockSpec((tk, k.shape[1]), lambda b, i, j: (ok + b * nk + j, 0)),
                  pl.BlockSpec((tk, v.shape[1]), lambda b, i, j: (ok + b * nk + j, 0))],
        out_specs=pl.BlockSpec((tq, 256), lambda b, i, j: (b * nq + i, 0)),
        out_shape=jax.ShapeDtypeStruct((B * S, 256), bf16),
        scratch_shapes=[pltpu.VMEM((n_acc, tq, LANES), f32), pltpu.VMEM((n_acc, tq, LANES), f32)],
        compiler_params=_cparams(("parallel", "parallel", "arbitrary")),
        name=name,
    )(sc, g, q, k, v)


def _dn_prep_kernel(x_ref, xp_ref, xn_ref, w_ref, bd_ref, q_ref, k_ref, v_ref, buf_ref, *,
                    tm, n_prompt_tiles, prompt_seq_tiles, sample_seq_tiles):
    i = pl.program_id(0)
    seq_tiles = jnp.where(i < n_prompt_tiles, prompt_seq_tiles, sample_seq_tiles)
    pos = jnp.where(i < n_prompt_tiles, i, i - n_prompt_tiles) % seq_tiles
    buf_ref[0:8, :] = jnp.where(pos == 0, 0.0, xp_ref[...])
    buf_ref[8:8 + tm, :] = x_ref[...]
    buf_ref[8 + tm:16 + tm, :] = jnp.where(pos == seq_tiles - 1, 0.0, xn_ref[...])
    pad = (DN_CONV - 1) // 2
    acc = jnp.zeros((tm, DN_QKV), f32)
    for t in range(DN_CONV):
        acc = acc + buf_ref[pl.ds(8 - pad + t, tm), :] * w_ref[t:t + 1, :]
    act = _silu(acc)
    bd = bd_ref[...]
    hk = DN_HEADS * DN_DK
    q, k = act[:, :hk], act[:, hk:2 * hk]
    q_ref[...] = q * lax.rsqrt(_group_sum64(q * q, bd) + NORM_EPS) * (DN_DK ** -0.5)
    k_ref[...] = k * lax.rsqrt(_group_sum64(k * k, bd) + NORM_EPS)
    v_ref[...] = act[:, 2 * hk:]


def _dn_prep(x, w, bd, n_prompt_tiles, prompt_seq_tiles, sample_seq_tiles, tm):
    T = x.shape[0]
    nb8 = tm // 8
    o = (pl.BlockSpec((tm, 256), lambda i: (i, 0)), jax.ShapeDtypeStruct((T, 256), f32))
    return pl.pallas_call(
        functools.partial(_dn_prep_kernel, tm=tm, n_prompt_tiles=n_prompt_tiles,
                          prompt_seq_tiles=prompt_seq_tiles, sample_seq_tiles=sample_seq_tiles),
        grid=(T // tm,),
        in_specs=[pl.BlockSpec((tm, DN_QKV), lambda i: (i, 0)),
                  pl.BlockSpec((8, DN_QKV), lambda i: (jnp.maximum(i * nb8 - 1, 0), 0)),
                  pl.BlockSpec((8, DN_QKV), lambda i: (jnp.minimum((i + 1) * nb8, T // 8 - 1), 0)),
                  _resident((8, DN_QKV)), _resident((256, 256))],
        out_specs=[o[0]] * 3,
        out_shape=[o[1]] * 3,
        scratch_shapes=[pltpu.VMEM((tm + 16, DN_QKV), f32)],
        compiler_params=_cparams(("parallel",)),
        name="dn_prep",
    )(x, x, x, w, bd)


def _dn_prep_chunks(chains, bd, bdb):
    n = 2 * DN_CHUNK
    R = range(len(chains))
    rid = lax.broadcasted_iota(jnp.int32, (n, n), 0)
    lane = lax.broadcasted_iota(jnp.int32, (1, LANES), 1)
    low = lane < 64
    on = bd > 0.0

    def fold(x):
        return x + pltpu.roll(x, 64, 1)

    g_rows, beta_rows = [], []
    for (q, k, v, a_row, b_row, eal_row, dtb_row, mi, ms, um) in chains:
        xa = a_row + dtb_row
        g_rows.append(-eal_row * (jnp.maximum(xa, 0.0) + jnp.log1p(jnp.exp(-jnp.abs(xa)))))
        beta_rows.append(jax.nn.sigmoid(b_row))
    g8 = [jnp.broadcast_to(g, (8, n)) for g in g_rows]
    gc_rows = [_split3_dot(g8[c], chains[c][9])[0:1] for c in R]
    gsum_rows = [_split3_dot(g8[c], bdb)[0:1] for c in R]
    cols = [jnp.where(rid == 0, gc_rows[c], jnp.where(rid == 1, beta_rows[c], 0.0)).T for c in R]
    cg = [jnp.broadcast_to(cols[c][:, 0:1], (n, n)) for c in R]
    cb = [jnp.broadcast_to(cols[c][:, 1:2], (n, n)) for c in R]
    decay = [jnp.exp(jnp.where(chains[c][7] > 0.0, cg[c] - gc_rows[c], NEG_BIG)) for c in R]
    eg = [jnp.exp(cg[c]) for c in R]
    k4 = [jnp.tile(chains[c][1], (2, 1)) * bd for c in R]
    kb = [k4[c] * cb[c] for c in R]
    k4b = [k4[c].astype(bf16) for c in R]
    a_mat = [_dot_nt(kb[c].astype(bf16), k4b[c]) * decay[c] * chains[c][8] for c in R]
    x = [jnp.where(low, fold(jnp.tile(chains[c][2], (2, 1)) * bd * cb[c]), fold(kb[c] * eg[c])) for c in R]
    b_mat = [-a for a in a_mat]
    for t in range(6):
        bb = [b.astype(bf16) for b in b_mat]
        x = [x[c] + _dot(bb[c], x[c].astype(bf16)) for c in R]
        if t < 5:
            b_mat = [_dot(bb[c], bb[c]) for c in R]
    q4 = [jnp.tile(chains[c][0], (2, 1)) * bd for c in R]
    intra = [(_dot_nt(q4[c].astype(bf16), k4b[c]) * decay[c]).astype(bf16) for c in R]
    out = []
    for c in R:
        xr = pltpu.roll(x[c], 64, 1)
        u_bd = jnp.where(low, x[c], xr) * bd
        w_bd = (jnp.where(low, xr, x[c]) * bd).astype(bf16)
        ekd = jnp.exp(jnp.where(on, gsum_rows[c] - cg[c], 0.0))
        out.append((u_bd, w_bd, (q4[c] * eg[c]).astype(bf16), intra[c], (k4[c] * ekd).T.astype(bf16),
                    jnp.exp(gsum_rows[c])))
    return out


def _dn_step(s, pre):
    u_bd, w_bd, qg, intra, kd_t, egl = pre
    sb = s.astype(bf16)
    v_new = u_bd - _dot(w_bd, sb)
    vb = v_new.astype(bf16)
    o_bd = _dot(qg, sb) + _dot(intra, vb)
    o = o_bd[0:64] + o_bd[64:128]
    return s * egl + _dot(kd_t, vb), o


def _dn_chunk_kernel(qf_ref, kf_ref, vf_ref, gf_ref, qb_ref, kb_ref, vb_ref, gb_ref, prm_ref,
                     mif_ref, msf_ref, mib_ref, msb_ref, bd_ref, of_ref, ob_ref, sf_ref, sb_ref, *, nch):
    @pl.when(pl.program_id(1) == 0)
    def _():
        sf_ref[...] = jnp.zeros(sf_ref.shape, f32)
        sb_ref[...] = jnp.zeros(sb_ref.shape, f32)

    C = DN_CHUNK
    bd = bd_ref[...]
    bdb = bd.astype(bf16)
    mif, mib = mif_ref[...], mib_ref[...]
    umf, umb = mib.astype(bf16), mif.astype(bf16)
    msf, msb = msf_ref[...], msb_ref[...]
    chains = []
    for (q_ref, k_ref, v_ref, g_ref, ga, gb_, pa, mi, ms, um) in (
            (qf_ref, kf_ref, vf_ref, gf_ref, 0, 1, 0, mif, msf, umf),
            (qb_ref, kb_ref, vb_ref, gb_ref, 2, 3, 2, mib, msb, umb)):
        for c in range(nch):
            r = slice(c * C, (c + 1) * C)
            for p in range(2):
                l = slice(p * LANES, (p + 1) * LANES)
                chains.append((q_ref[r, l], k_ref[r, l], v_ref[r, l], g_ref[c, ga:ga + 1, l], g_ref[c, gb_:gb_ + 1, l],
                               prm_ref[pa:pa + 1, l], prm_ref[pa + 1:pa + 2, l], mi, ms, um))
    pre = _dn_prep_chunks(chains, bd, bdb)
    half = 2 * nch
    for p in range(2):
        l = slice(p * LANES, (p + 1) * LANES)
        sf, sb = sf_ref[p], sb_ref[p]
        for c in range(nch):
            sf, o = _dn_step(sf, pre[2 * c + p])
            of_ref[c * C:(c + 1) * C, l] = o
            cr = nch - 1 - c
            sb, o = _dn_step(sb, pre[half + 2 * cr + p])
            ob_ref[cr * C:(cr + 1) * C, l] = o
        sf_ref[p] = sf
        sb_ref[p] = sb


def _dn_chunk(q, k, v, gates, prm, masks, *, B, S, off, nch=4):
    R = DN_CHUNK * nch
    N = S // R
    oc = off // R
    fwd = lambda b, c: (oc + b * N + c, 0)
    bwd = lambda b, c: (oc + b * N + N - 1 - c, 0)
    tokf = pl.BlockSpec((R, 256), fwd)
    tokb = pl.BlockSpec((R, 256), bwd)
    gf = pl.BlockSpec((nch, 8, 256), lambda b, c: (oc + b * N + c, 0, 0))
    gb = pl.BlockSpec((nch, 8, 256), lambda b, c: (oc + b * N + N - 1 - c, 0, 0))
    sq = _resident((LANES, LANES))
    return pl.pallas_call(
        functools.partial(_dn_chunk_kernel, nch=nch),
        grid=(B, N),
        in_specs=[tokf, tokf, tokf, gf, tokb, tokb, tokb, gb, _resident((8, 256)), sq, sq, sq, sq, sq],
        out_specs=[pl.BlockSpec((R, 256), lambda b, c: (b * N + c, 0)),
                   pl.BlockSpec((R, 256), lambda b, c: (b * N + N - 1 - c, 0))],
        out_shape=[jax.ShapeDtypeStruct((B * S, 256), f32)] * 2,
        scratch_shapes=[pltpu.VMEM((2, LANES, LANES), f32), pltpu.VMEM((2, LANES, LANES), f32)],
        compiler_params=_cparams(("parallel", "arbitrary")),
        name="dn_chunk",
    )(q, k, v, gates, q, k, v, gates, prm, *masks)


def _dil_kernel(q_ref, k_ref, v_ref, kp0, kn0, kp1, kn1, kp2, kn2, vp0, vn0, vp1, vn1, vp2, vn2,
                o_ref, os_ref, ls_ref, *, ts, nb, n_prompt_tiles, prompt_seq_tiles, sample_seq_tiles):
    i = pl.program_id(0)
    in_prompt = i < n_prompt_tiles
    seq_tiles = jnp.where(in_prompt, prompt_seq_tiles, sample_seq_tiles)
    pos = jnp.where(in_prompt, i, i - n_prompt_tiles) % seq_tiles
    first, last = pos == 0, pos == seq_tiles - 1
    halos = ((kp0, kn0, vp0, vn0), (kp1, kn1, vp1, vn1), (kp2, kn2, vp2, vn2))
    lane = lax.broadcasted_iota(jnp.int32, (1, LANES), 1)
    low = lane < DIL_HD
    R = DIL_RADIUS
    items = []
    for gi, (_, d) in enumerate(DIL_GROUPS):
        n = ts // d
        qb = min(LANES, n)
        for r in range(d):
            for j in range(n // qb):
                for pr in range(2):
                    items.append((gi, d, n, qb, r, j, pr))

    def load_keys(main, prv, nxt, gi, d, n, qb, r, t0, pr):
        parts = []
        lo, hi = t0 - R, t0 + qb + R
        if lo < 0:
            parts.append(prv[pr, pl.ds(r + d * (lo + R), -lo, stride=d), :])
        a, b = max(lo, 0), min(hi, n)
        parts.append(main[2 * gi + pr, pl.ds(r + d * a, b - a, stride=d), :])
        if hi > n:
            parts.append(nxt[pr, pl.ds(r, hi - n, stride=d), :])
        return jnp.concatenate(parts, axis=0).astype(bf16)

    masks = {}

    def mask_for(qb, n, t0):
        key = (qb, n, t0)
        if key not in masks:
            nkeys = qb + 2 * R
            row = lax.broadcasted_iota(jnp.int32, (qb, nkeys), 0)
            col = lax.broadcasted_iota(jnp.int32, (qb, nkeys), 1)
            valid = jnp.abs(col - R - row) <= R
            if t0 - R < 0:
                valid = valid & ((col >= R - t0) | jnp.logical_not(first))
            if t0 + qb + R > n:
                valid = valid & ((col < n + R - t0) | jnp.logical_not(last))
            masks[key] = valid
        return masks[key]

    for b0 in range(0, len(items), nb):
        batch = items[b0:b0 + nb]
        qs, ks, vs, vl = [], [], [], []
        for (gi, d, n, qb, r, j, pr) in batch:
            t0 = j * qb
            kp, kn, vp, vn = halos[gi]
            qs.append(q_ref[2 * gi + pr, pl.ds(r + d * t0, qb, stride=d), :].astype(bf16))
            ks.append(load_keys(k_ref, kp, kn, gi, d, n, qb, r, t0, pr))
            vs.append(load_keys(v_ref, vp, vn, gi, d, n, qb, r, t0, pr))
            vl.append(mask_for(qb, n, t0))
        U = [(i_, hh) for i_ in range(len(batch)) for hh in range(2)]
        qm = [jnp.where(low if hh == 0 else ~low, qs[i_], jnp.zeros_like(qs[i_])) for (i_, hh) in U]
        sc = [jnp.where(vl[i_], _dot_nt(qm[u], ks[i_]), NEG_BIG) for u, (i_, hh) in enumerate(U)]
        mm = [jnp.max(x, axis=1, keepdims=True) for x in sc]
        ee = [jnp.exp(sc[u] - mm[u]) for u in range(len(U))]
        dd = [jnp.sum(x, axis=1, keepdims=True) for x in ee]
        pv = [_dot((ee[u] / dd[u]).astype(bf16), vs[i_]) for u, (i_, hh) in enumerate(U)]
        ll = [mm[u] + jnp.log(dd[u]) for u in range(len(U))]
        for i_, (gi, d, n, qb, r, j, pr) in enumerate(batch):
            dst = pl.ds(r + d * j * qb, qb, stride=d)
            os_ref[2 * gi + pr, dst, :] = jnp.where(low, pv[2 * i_], pv[2 * i_ + 1])
            ls_ref[2 * gi + pr, dst, :] = jnp.where(low, ll[2 * i_], ll[2 * i_ + 1])
    for pr in range(2):
        l0, l1, l2 = ls_ref[pr], ls_ref[2 + pr], ls_ref[4 + pr]
        mx = jnp.maximum(jnp.maximum(l0, l1), l2)
        e0, e1, e2 = jnp.exp(l0 - mx), jnp.exp(l1 - mx), jnp.exp(l2 - mx)
        y = (e0 * os_ref[pr] + e1 * os_ref[2 + pr] + e2 * os_ref[4 + pr]) / (e0 + e1 + e2)
        o_ref[:, pr * LANES:(pr + 1) * LANES] = y.astype(o_ref.dtype)


def _dilated(q, k, v, n_prompt_tiles, prompt_seq_tiles, sample_seq_tiles, ts, nb=8):
    T = q.shape[1]
    main = pl.BlockSpec((6, ts, LANES), lambda i: (0, i, 0))
    halo_specs = []
    for gi, (_, d) in enumerate(DIL_GROUPS):
        h = DIL_RADIUS * d
        per = ts // h
        halo_specs.append(pl.BlockSpec((2, h, LANES),
                                       lambda i, per=per, gi=gi: (gi, jnp.maximum(i * per - 1, 0), 0)))
        halo_specs.append(pl.BlockSpec((2, h, LANES),
                                       lambda i, per=per, gi=gi, h=h: (gi, jnp.minimum((i + 1) * per, T // h - 1), 0)))
    return pl.pallas_call(
        functools.partial(_dil_kernel, ts=ts, nb=nb, n_prompt_tiles=n_prompt_tiles,
                          prompt_seq_tiles=prompt_seq_tiles, sample_seq_tiles=sample_seq_tiles),
        grid=(T // ts,),
        in_specs=[main, main, main] + halo_specs + halo_specs,
        out_specs=pl.BlockSpec((ts, 256), lambda i: (i, 0)),
        out_shape=jax.ShapeDtypeStruct((T, 256), bf16),
        scratch_shapes=[pltpu.VMEM((6, ts, LANES), f32), pltpu.VMEM((6, ts, LANES), f32)],
        compiler_params=_cparams(("parallel",)),
        name="dilated",
    )(q, k, v, *([k] * 6), *([v] * 6))


def _merge_kernel(x_ref, g_ref, ya_ref, yb_ref, of_ref, ob_ref, z_ref, dng_ref, bd_ref, yd_ref,
                  wg_ref, wb_ref, wo_ref, out_ref):
    x = x_ref[...]
    hb = _rms(x, g_ref[...]).astype(bf16)
    o = of_ref[...] + ob_ref[...]
    ms = _group_sum64(o * o, bd_ref[...]) * (1.0 / DN_DV)
    yc = o * lax.rsqrt(ms + NORM_EPS) * dng_ref[...] * _silu(z_ref[...].astype(f32))
    ys = (ya_ref[...], yb_ref[...], yc.astype(bf16), yd_ref[...])
    merged = jnp.zeros_like(x)
    for n in range(4):
        merged = merged + jax.nn.sigmoid(_dot(hb, wg_ref[n])) * _dot(ys[n], wb_ref[n])
    out_ref[...] = x + _dot(merged.astype(bf16), wo_ref[...])


def _merge(x, g, ya, yb, of, ob, z, dng, bd, yd, wg, wb, wo, tm=512):
    T = x.shape[0]
    t256 = pl.BlockSpec((tm, 256), lambda i: (i, 0))
    return pl.pallas_call(
        _merge_kernel,
        grid=(T // tm,),
        in_specs=[pl.BlockSpec((tm, D_MODEL), lambda i: (i, 0)), _resident((1, D_MODEL)),
                  t256, t256, t256, t256, t256, _resident((1, 256)), _resident((256, 256)), t256,
                  _resident((4, D_MODEL, D_MODEL)), _resident((4, 256, D_MODEL)), _resident((D_MODEL, D_MODEL))],
        out_specs=pl.BlockSpec((tm, D_MODEL), lambda i: (i, 0)),
        out_shape=jax.ShapeDtypeStruct((T, D_MODEL), f32),
        compiler_params=_cparams(("parallel",)),
        name="merge",
    )(x, g, ya, yb, of, ob, z, dng, bd, yd, wg, wb, wo)


def _rope_tables(smax):
    pos = jnp.arange(smax, dtype=f32)[:, None]

    def cs(d):
        inv = ROPE_THETA ** (-jnp.arange(0, d, 2, dtype=f32) / d)
        ang = pos * inv[None, :]
        return jnp.cos(ang), jnp.sin(ang)

    def head_pattern(d):
        c, s = cs(d)
        z = jnp.zeros_like(s)
        return jnp.concatenate([c, c], 1), jnp.concatenate([-s, z], 1), jnp.concatenate([z, s], 1)

    c32, a32, b32 = head_pattern(32)
    c64, a64, b64 = head_pattern(64)
    one = jnp.ones((smax, 64), f32)
    z64 = jnp.zeros((smax, 64), f32)
    z32 = jnp.zeros((smax, 32), f32)
    set_a = [jnp.tile(t, (1, 4)) for t in (c32, a32, b32)]
    set_b = [jnp.concatenate([one, c32, z32], 1), jnp.concatenate([z64, a32, z32], 1),
             jnp.concatenate([z64, b32, z32], 1)]
    set_c = [jnp.tile(t, (1, 2)) for t in (c64, a64, b64)]
    return jnp.stack(set_a + set_b + set_c, 0)


def _head_pad(w, n_heads, width, lo, hi, at=0):
    k = w.shape[0]
    w = w.reshape(k, n_heads, width)[:, :, lo:hi]
    out = jnp.zeros((k, n_heads, LANES), w.dtype).at[:, :, at:at + hi - lo].set(w)
    return out.reshape(k, n_heads * LANES)


def _layer_weights(w, li):
    sizes = (MLA_Q_RANK, MLA_KV_RANK, MLA_ROPE, 256, 256, 256, DN_QKV, 4, 4, 4, 4, 256,
             DIL_COLS, DIL_COLS, DIL_COLS)
    offs = np.cumsum((0,) + sizes)
    win = w['w_in'][li]
    col = lambda n: win[:, offs[n]:offs[n + 1]]
    kr_pad = jnp.zeros((D_MODEL, LANES), f32).at[:, MLA_NOPE:MLA_NOPE + MLA_ROPE].set(col(2))
    gates = jnp.concatenate([col(7), col(8), col(9), col(10), jnp.zeros((D_MODEL, LANES - 16), f32)], 1)
    big = jnp.concatenate([col(0), col(1), kr_pad, col(3), col(4), _head_pad(col(5), 4, 64, 0, 64),
                           col(6), gates, col(11), col(12), col(13), col(14)], 1).astype(bf16)
    assert big.shape[1] == C_END
    ukv = w['mla_w_ukv'][li]
    out = dict(
        w_in=big,
        wuq=_head_pad(w['mla_w_uq'][li], 4, MLA_NOPE + MLA_ROPE, 0, MLA_NOPE + MLA_ROPE).astype(bf16),
        wuk=_head_pad(ukv, 4, MLA_NOPE + MLA_V, 0, MLA_NOPE).astype(bf16),
        wuv=_head_pad(ukv, 4, MLA_NOPE + MLA_V, MLA_NOPE, MLA_NOPE + MLA_V).astype(bf16),
    )
    for n in ('ff1_w1', 'ff1_w3', 'ff1_w2', 'ff2_w1', 'ff2_w3', 'ff2_w2', 'w_gate', 'w_branch', 'w_out',
              'ple_gate', 'ple_proj'):
        out[n] = w[n][li].astype(bf16)
    return out


def _dn_masks(n_heads):
    idx = np.arange(n_heads * DN_CHUNK)
    same = (idx[:, None] // DN_CHUNK) == (idx[None, :] // DN_CHUNK)
    i, j = idx[:, None] % DN_CHUNK, idx[None, :] % DN_CHUNK
    mk = lambda m: jnp.asarray((same & m).astype(np.float32))
    return mk(i >= j), mk(i > j), mk(i <= j), mk(i < j), mk(np.ones_like(same))


def _seq_tile(s, pref):
    return min(s, pref)


def kernel(x_prompt, x_sample, p_prompt, p_sample, norm_ff1, ff1_w1, ff1_w3, ff1_w2, norm_mix, w_in, mla_q_norm, mla_kv_norm, mla_w_uq, mla_w_ukv, diff_lambda, diff_subln, dn_conv, dn_a_log, dn_dt_bias, dn_out_norm, w_branch, w_gate, w_out, norm_ff2, ff2_w1, ff2_w3, ff2_w2, norm_ple, ple_gate, ple_proj, norm_final):
    w = dict(norm_ff1=norm_ff1, ff1_w1=ff1_w1, ff1_w3=ff1_w3, ff1_w2=ff1_w2, norm_mix=norm_mix, w_in=w_in,
             mla_q_norm=mla_q_norm, mla_kv_norm=mla_kv_norm, mla_w_uq=mla_w_uq, mla_w_ukv=mla_w_ukv,
             diff_lambda=diff_lambda, diff_subln=diff_subln, dn_conv=dn_conv, dn_a_log=dn_a_log,
             dn_dt_bias=dn_dt_bias, dn_out_norm=dn_out_norm, w_branch=w_branch, w_gate=w_gate, w_out=w_out,
             norm_ff2=norm_ff2, ff2_w1=ff2_w1, ff2_w3=ff2_w3, ff2_w2=ff2_w2, norm_ple=norm_ple,
             ple_gate=ple_gate, ple_proj=ple_proj)
    depth = w_in.shape[0]
    BP, SP, _ = x_prompt.shape
    BS, SS, _ = x_sample.shape
    TP, TS = BP * SP, BS * SS
    T = TP + TS
    groups = ((BP, SP, 0), (BS, SS, TP))
    tm = min(512, SP, SS)
    assert TP % tm == 0 and TS % tm == 0 and SP % tm == 0 and SS % tm == 0

    xs = [x_prompt.reshape(TP, D_MODEL), x_sample.reshape(TS, D_MODEL)]
    ps = [p_prompt.reshape(depth, TP, PLE_DIM), p_sample.reshape(depth, TS, PLE_DIM)]
    npt = TP // tm

    tab = _rope_tables(max(SP, SS))
    masks = _dn_masks(2)
    bd_bf = _dn_masks(DN_HEADS)[4].astype(bf16)
    ones_pat = jnp.tile(jnp.concatenate([jnp.zeros((1, 64), f32), jnp.ones((1, 64), f32)], 1), (1, 4))
    row = lambda v: v.reshape(1, -1).astype(f32)
    mla_units = tuple((h, h, h) for h in range(MLA_HEADS))
    diff_units = tuple((hm, hm // 4, hm // 2) for hm in range(2 * DIFF_HEADS))
    no_sc = jnp.zeros((2,), f32)
    no_g = jnp.zeros((1, LANES), f32)

    for li in range(depth):
        lw = _layer_weights(w, li)
        x = _ffn(xs, row(norm_ff1[li]), lw['ff1_w1'], lw['ff1_w3'], lw['ff1_w2'], n_prompt_tiles=npt, tm=tm)

        (mq, mk, mv, dq, dk, dv, nqkv, ng, nz, lq, lk, lv) = _inproj(
            x, row(norm_mix[li]), lw['w_in'], row(mla_q_norm[li]), row(mla_kv_norm[li]),
            lw['wuq'], lw['wuk'], lw['wuv'], ones_pat, tab, TP // tm, SP // tm, SS // tm, tm)

        lf = diff_lambda[li].astype(f32)
        lambda_init = 0.8 - 0.6 * math.exp(-0.3 * li)
        lam = jnp.exp(jnp.sum(lf[0] * lf[1])) - jnp.exp(jnp.sum(lf[2] * lf[3])) + lambda_init
        sc = jnp.stack([lam, jnp.asarray(1.0 - lambda_init, f32)]).astype(f32)
        subln = jnp.tile(row(diff_subln[li]), (1, 2))
        ya, yb = [], []
        for (B, S, off) in groups:
            tq = _seq_tile(S, 2048)
            ya.append(_flash(mq, mk, mv, no_sc, no_g, B=B, S=S, off=off, units=mla_units, diff=False,
                             tq=tq, tk=_seq_tile(S, 1024), name="mla_attn"))
            yb.append(_flash(dq, dk, dv, sc, subln, B=B, S=S, off=off, units=diff_units, diff=True,
                             tq=tq, tk=_seq_tile(S, 256), name="diff_attn"))
        ya, yb = jnp.concatenate(ya, 0), jnp.concatenate(yb, 0)

        conv_w = jnp.concatenate([dn_conv[li].astype(f32), jnp.zeros((8 - DN_CONV, DN_QKV), f32)], 0)
        nq, nk, nv = _dn_prep(nqkv, conv_w, bd_bf, TP // tm, SP // tm, SS // tm, tm)
        gates = ng[:, :16].reshape(T // DN_CHUNK, DN_CHUNK, 4, DN_HEADS)
        gates = jnp.transpose(gates, (0, 2, 3, 1)).reshape(T // DN_CHUNK, 4, 4 * DN_CHUNK)
        gates = jnp.concatenate([gates, jnp.zeros_like(gates)], 1)
        al, dtb = dn_a_log[li].astype(f32), dn_dt_bias[li].astype(f32)
        rep = lambda v: jnp.repeat(v, DN_CHUNK)[None, :]
        prm = jnp.concatenate([rep(jnp.exp(al[0])), rep(dtb[0]), rep(jnp.exp(al[1])), rep(dtb[1]),
                               jnp.zeros((4, 4 * DN_CHUNK), f32)], 0)
        of, ob = [], []
        for (B, S, off) in groups:
            f_, b_ = _dn_chunk(nq, nk, nv, gates, prm, masks, B=B, S=S, off=off)
            of.append(f_)
            ob.append(b_)
        of, ob = jnp.concatenate(of, 0), jnp.concatenate(ob, 0)

        ts = min(1024, SP, SS)
        yd = _dilated(lq, lk, lv, TP // ts, SP // ts, SS // ts, ts)

        x = _merge(x, row(norm_mix[li]), ya, yb, of, ob, nz, jnp.tile(row(dn_out_norm[li]), (1, 4)), bd_bf,
                   yd, lw['w_gate'], lw['w_branch'], lw['w_out'], tm=tm)
        last = li == depth - 1
        ple = (ps, li, row(norm_ple[li]), lw['ple_gate'], lw['ple_proj'], row(norm_final))
        x = _ffn([x], row(norm_ff2[li]), lw['ff2_w1'], lw['ff2_w3'], lw['ff2_w2'], n_prompt_tiles=npt, tm=tm,
                 ple=ple, final=last, split_out=last)
        xs = [x]

    return (x[0].reshape(BP, SP, D_MODEL), x[1].reshape(BS, SS, D_MODEL))
```

```python
import functools
import math

import numpy as np
import jax
import jax.numpy as jnp
from jax import lax
from jax.experimental import pallas as pl
from jax.experimental.pallas import tpu as pltpu

f32 = jnp.float32
bf16 = jnp.bfloat16

D_MODEL = 1024
PLE_DIM = 256
D_FF = 2816
ROPE_THETA = 10000.0
NORM_EPS = 1e-6
NEG_BIG = -1e30
LOG2E = math.log2(math.e)

MLA_HEADS = 4
MLA_Q_RANK = 256
MLA_KV_RANK = 128
MLA_NOPE = 64
MLA_ROPE = 32
MLA_V = 64

DIFF_HEADS = 4
DIFF_HD = 32
DIFF_VD = 64

DN_HEADS = 4
DN_DK = 64
DN_DV = 64
DN_CONV = 5
DN_CHUNK = 64
DN_QKV = DN_HEADS * (2 * DN_DK + DN_DV)

DIL_GROUPS = ((128, 1), (512, 4), (2048, 16))
DIL_HEADS = 4
DIL_HD = 64
DIL_RADIUS = 64
DIL_COLS = len(DIL_GROUPS) * DIL_HEADS * DIL_HD

LANES = 128
VMEM_LIMIT = 56 * 1024 * 1024

C_CQKV = 0
C_BQ = 512
C_BK = 768
C_BV = 1024
C_DNQKV = 1536
C_DNG = 2304
C_DNZ = 2432
C_DQ = 2688
C_DK = 3456
C_DV = 4224
C_END = 4992


def _cparams(sem):
    return pltpu.CompilerParams(dimension_semantics=sem, vmem_limit_bytes=VMEM_LIMIT)


def _resident(shape):
    nd = len(shape)
    return pl.BlockSpec(shape, lambda *_: (0,) * nd, pipeline_mode=pl.Buffered(1))


def _dot(a, b):
    return jnp.dot(a, b, preferred_element_type=f32)


def _dot_nt(a, b):
    return lax.dot_general(a, b, (((1,), (1,)), ((), ())), preferred_element_type=f32)


def _rms(x, g):
    ms = jnp.mean(x * x, axis=-1, keepdims=True)
    return x * lax.rsqrt(ms + NORM_EPS) * g


def _silu(x):
    return x * jax.nn.sigmoid(x)


def _split3_dot(x, m):
    x1 = x.astype(bf16)
    r1 = x - x1.astype(f32)
    x2 = r1.astype(bf16)
    x3 = (r1 - x2.astype(f32)).astype(bf16)
    return _dot(x1, m) + _dot(x2, m) + _dot(x3, m)


def _group_sum64(x, bd):
    return _split3_dot(x, bd)


def _rope128(x, c, s1, s2, half):
    return x * c + pltpu.roll(x, LANES - half, 1) * s1 + pltpu.roll(x, half, 1) * s2


def _ffn_kernel(*refs, fc, n_in, ple, final, n_out, n_prompt_tiles):
    refs = list(refs)
    take = lambda k: [refs.pop(0) for _ in range(k)]
    x_refs = take(n_in)
    g_ref, w1_ref, w3_ref, w2_ref = take(4)
    if ple:
        p_refs = take(2)
        gp_ref, wg_ref, wp_ref, gf_ref = take(4)
    o_refs = take(n_out)
    in_prompt = pl.program_id(0) < n_prompt_tiles

    def pick(rs):
        return rs[0][...] if len(rs) == 1 else jnp.where(in_prompt, rs[0][...], rs[1][...])

    x = pick(x_refs)
    xn = _rms(x, g_ref[...]).astype(bf16)
    y = jnp.zeros_like(x)
    for c in range(D_FF // fc):
        a = _dot(xn, w1_ref[:, c * fc:(c + 1) * fc])
        b = _dot(xn, w3_ref[:, c * fc:(c + 1) * fc])
        y = y + _dot((_silu(a) * b).astype(bf16), w2_ref[c * fc:(c + 1) * fc, :])
    y = x + 0.5 * y
    if ple:
        yn = _rms(y, gp_ref[...]).astype(bf16)
        gate = jax.nn.sigmoid(_dot(yn, wg_ref[...]))
        y = y + gate * _dot(pick(p_refs).astype(bf16), wp_ref[...])
        if final:
            y = _rms(y, gf_ref[...])
    if n_out == 1:
        o_refs[0][...] = y
    else:
        @pl.when(in_prompt)
        def _():
            o_refs[0][...] = y

        @pl.when(jnp.logical_not(in_prompt))
        def _():
            o_refs[1][...] = y


def _ffn(xs, g, w1, w3, w2, *, n_prompt_tiles, tm, ple=None, final=False, split_out=False, fc=256):
    n_in = len(xs)
    tp = n_prompt_tiles * tm
    T = sum(x.shape[0] for x in xs)

    def tok_specs(width, dual, lead=None):
        def mk(fn):
            if lead is None:
                return pl.BlockSpec((tm, width), lambda i: (fn(i), 0))
            return pl.BlockSpec((None, tm, width), lambda i: (lead, fn(i), 0))
        if not dual:
            return [mk(lambda i: i)]
        return [mk(lambda i: jnp.minimum(i, n_prompt_tiles - 1)), mk(lambda i: jnp.maximum(i - n_prompt_tiles, 0))]

    args = list(xs) + [g, w1, w3, w2]
    in_specs = tok_specs(D_MODEL, n_in == 2) + [_resident((1, D_MODEL)), _resident((D_MODEL, D_FF)),
                                                _resident((D_MODEL, D_FF)), _resident((D_FF, D_MODEL))]
    if ple is not None:
        ps, li, gp, wg, wp, gf = ple
        args += list(ps) + [gp, wg, wp, gf]
        in_specs += tok_specs(PLE_DIM, True, lead=li) + [_resident((1, D_MODEL)), _resident((D_MODEL, D_MODEL)),
                                                         _resident((PLE_DIM, D_MODEL)), _resident((1, D_MODEL))]
    if split_out:
        out_specs = [pl.BlockSpec((tm, D_MODEL), lambda i: (jnp.minimum(i, n_prompt_tiles - 1), 0)),
                     pl.BlockSpec((tm, D_MODEL), lambda i: (jnp.maximum(i - n_prompt_tiles, 0), 0))]
        out_shape = [jax.ShapeDtypeStruct((tp, D_MODEL), f32), jax.ShapeDtypeStruct((T - tp, D_MODEL), f32)]
    else:
        out_specs = pl.BlockSpec((tm, D_MODEL), lambda i: (i, 0))
        out_shape = jax.ShapeDtypeStruct((T, D_MODEL), f32)
    return pl.pallas_call(
        functools.partial(_ffn_kernel, fc=fc, n_in=n_in, ple=ple is not None, final=final,
                          n_out=2 if split_out else 1, n_prompt_tiles=n_prompt_tiles),
        grid=(T // tm,),
        in_specs=in_specs,
        out_specs=out_specs,
        out_shape=out_shape,
        compiler_params=_cparams(("arbitrary",) if split_out else ("parallel",)),
        name="ffn_ple" if ple is not None else "ffn",
    )(*args)


def _inproj_kernel(x_ref, g_ref, w_ref, qn_ref, kvn_ref, wuq_ref, wuk_ref, wuv_ref, ones_ref, tab_ref,
                   mq_ref, mk_ref, mv_ref, dq_ref, dk_ref, dv_ref, nqkv_ref, ng_ref, nz_ref,
                   lq_ref, lk_ref, lv_ref):
    hb = _rms(x_ref[...], g_ref[...]).astype(bf16)

    def proj(lo, hi):
        return _dot(hb, w_ref[:, lo:hi])

    def tabs(s):
        return tab_ref[3 * s], tab_ref[3 * s + 1], tab_ref[3 * s + 2]

    ones_pat = ones_ref[...]

    c = proj(C_CQKV, C_CQKV + 512)
    cq, ckv, krp = c[:, :MLA_Q_RANK], c[:, MLA_Q_RANK:MLA_Q_RANK + MLA_KV_RANK], c[:, 384:512]
    q = _dot(_rms(cq, qn_ref[...]).astype(bf16), wuq_ref[...])
    kvn = _rms(ckv, kvn_ref[...]).astype(bf16)
    kn = _dot(kvn, wuk_ref[...])
    cb, s1b, s2b = tabs(1)
    q_scale = (MLA_NOPE + MLA_ROPE) ** -0.5 * LOG2E
    kr = _rope128(krp, cb, s1b, s2b, MLA_ROPE // 2)
    for h in range(MLA_HEADS):
        sl = slice(h * LANES, (h + 1) * LANES)
        mq_ref[:, sl] = (_rope128(q[:, sl], cb, s1b, s2b, MLA_ROPE // 2) * q_scale).astype(bf16)
        mk_ref[:, sl] = (kn[:, sl] + kr).astype(bf16)
    mv_ref[...] = (_dot(kvn, wuv_ref[...]) + ones_pat).astype(bf16)

    ca, s1a, s2a = tabs(0)
    lane = lax.broadcasted_iota(jnp.int32, (1, LANES), 1)
    bq = proj(C_BQ, C_BQ + 256)
    bk = proj(C_BK, C_BK + 256)
    for ch in range(2):
        sl = slice(ch * LANES, (ch + 1) * LANES)
        qr = _rope128(bq[:, sl], ca, s1a, s2a, DIFF_HD // 2) * (DIFF_HD ** -0.5 * LOG2E)
        for slot in range(4):
            hm = ch * 4 + slot
            dq_ref[:, hm * LANES:(hm + 1) * LANES] = jnp.where(lane // DIFF_HD == slot, qr, 0.0).astype(bf16)
        dk_ref[:, sl] = _rope128(bk[:, sl], ca, s1a, s2a, DIFF_HD // 2).astype(bf16)
    dv_ref[...] = (proj(C_BV, C_BV + 512) + ones_pat).astype(bf16)

    nqkv_ref[...] = proj(C_DNQKV, C_DNQKV + DN_QKV)
    ng_ref[...] = proj(C_DNG, C_DNG + LANES)
    nz_ref[...] = proj(C_DNZ, C_DNZ + 256).astype(bf16)

    cc, s1c, s2c = tabs(2)
    lq = proj(C_DQ, C_DQ + DIL_COLS)
    lk = proj(C_DK, C_DK + DIL_COLS)
    lv = proj(C_DV, C_DV + DIL_COLS)
    for ch in range(DIL_COLS // LANES):
        sl = slice(ch * LANES, (ch + 1) * LANES)
        lq_ref[ch] = _rope128(lq[:, sl], cc, s1c, s2c, DIL_HD // 2) * (DIL_HD ** -0.5)
        lk_ref[ch] = _rope128(lk[:, sl], cc, s1c, s2c, DIL_HD // 2)
        lv_ref[ch] = lv[:, sl]


def _inproj(x, g, w, qn, kvn, wuq, wuk, wuv, ones_pat, tab, n_prompt_tiles, prompt_pos_tiles, sample_pos_tiles,
            tm):
    T = x.shape[0]

    def pos_map(i):
        return (0, jnp.where(i < n_prompt_tiles, i % prompt_pos_tiles, (i - n_prompt_tiles) % sample_pos_tiles), 0)

    def tok(w_, dt):
        return pl.BlockSpec((tm, w_), lambda i: (i, 0)), jax.ShapeDtypeStruct((T, w_), dt)

    dil = (pl.BlockSpec((DIL_COLS // LANES, tm, LANES), lambda i: (0, i, 0)),
           jax.ShapeDtypeStruct((DIL_COLS // LANES, T, LANES), f32))
    outs = [tok(512, bf16), tok(512, bf16), tok(512, bf16),
            tok(1024, bf16), tok(256, bf16), tok(512, bf16),
            tok(DN_QKV, f32), tok(LANES, f32), tok(256, bf16),
            dil, dil, dil]
    return pl.pallas_call(
        _inproj_kernel,
        grid=(T // tm,),
        in_specs=[pl.BlockSpec((tm, D_MODEL), lambda i: (i, 0)),
                  _resident((1, D_MODEL)), _resident((D_MODEL, C_END)),
                  _resident((1, MLA_Q_RANK)), _resident((1, MLA_KV_RANK)),
                  _resident((MLA_Q_RANK, 512)), _resident((MLA_KV_RANK, 512)), _resident((MLA_KV_RANK, 512)),
                  _resident((1, 512)),
                  pl.BlockSpec((9, tm, LANES), pos_map)],
        out_specs=[o[0] for o in outs],
        out_shape=[o[1] for o in outs],
        compiler_params=_cparams(("parallel",)),
        name="inproj",
    )(x, g, w, qn, kvn, wuq, wuk, wuv, ones_pat, tab)


def _flash_kernel(sc_ref, g_ref, q_ref, k_ref, v_ref, o_ref, m_ref, acc_ref, *, units, diff, nk):
    ki = pl.program_id(2)

    @pl.when(ki == 0)
    def _():
        m_ref[...] = jnp.full(m_ref.shape, NEG_BIG, f32)
        acc_ref[...] = jnp.zeros(acc_ref.shape, f32)

    nc = k_ref.shape[0] // LANES
    for u, (qi, kc, vh) in enumerate(units):
        q = q_ref[:, qi * LANES:(qi + 1) * LANES]
        k = k_ref[:, kc * LANES:(kc + 1) * LANES]
        s = _dot_nt(q, k)
        m_prev = m_ref[u]
        mx = s[:, 0:LANES]
        for c in range(1, nc):
            mx = jnp.maximum(mx, s[:, c * LANES:(c + 1) * LANES])
        m_new = jnp.maximum(m_prev, jnp.max(mx, axis=1, keepdims=True))
        alpha = jnp.exp2(m_prev - m_new)
        p = jnp.concatenate([jnp.exp2(s[:, c * LANES:(c + 1) * LANES] - m_new).astype(bf16)
                             for c in range(nc)], axis=1)
        acc_ref[u] = alpha * acc_ref[u] + _dot(p, v_ref[:, vh * LANES:(vh + 1) * LANES])
        m_ref[u] = m_new

    @pl.when(ki == nk - 1)
    def _():
        lane = lax.broadcasted_iota(jnp.int32, (1, LANES), 1)
        low = lane < 64

        def normed(u):
            a = acc_ref[u]
            return a / pltpu.roll(a, 64, 1)

        heads = []
        if diff:
            lam, post = sc_ref[0], sc_ref[1]
            for h in range(DIFF_HEADS):
                o = normed(2 * h) - lam * normed(2 * h + 1)
                ms = jnp.sum(jnp.where(low, o * o, 0.0), axis=1, keepdims=True) * (1.0 / DIFF_VD)
                heads.append(o * lax.rsqrt(ms + NORM_EPS) * g_ref[...] * post)
        else:
            heads = [normed(u) for u in range(len(units))]
        for pr in range(2):
            pair = jnp.where(low, heads[2 * pr], pltpu.roll(heads[2 * pr + 1], 64, 1))
            o_ref[:, pr * LANES:(pr + 1) * LANES] = pair.astype(o_ref.dtype)


def _flash(q, k, v, sc, g, *, B, S, off, units, diff, tq, tk, name):
    nq, nk = S // tq, S // tk
    oq, ok = off // tq, off // tk
    n_acc = len(units)
    return pl.pallas_call(
        functools.partial(_flash_kernel, units=units, diff=diff, nk=nk),
        grid=(B, nq, nk),
        in_specs=[pl.BlockSpec(memory_space=pltpu.SMEM),
                  pl.BlockSpec((1, LANES), lambda b, i, j: (0, 0)),
                  pl.BlockSpec((tq, q.shape[1]), lambda b, i, j: (oq + b * nq + i, 0)),
                  pl.BlockSpec((tk, k.shape[1]), lambda b, i, j: (ok + b * nk + j, 0)),
                  pl.BlockSpec((tk, v.shape[1]), lambda b, i, j: (ok + b * nk + j, 0))],
        out_specs=pl.BlockSpec((tq, 256), lambda b, i, j: (b * nq + i, 0)),
        out_shape=jax.ShapeDtypeStruct((B * S, 256), bf16),
        scratch_shapes=[pltpu.VMEM((n_acc, tq, LANES), f32), pltpu.VMEM((n_acc, tq, LANES), f32)],
        compiler_params=_cparams(("parallel", "parallel", "arbitrary")),
        name=name,
    )(sc, g, q, k, v)


def _dn_prep_kernel(x_ref, xp_ref, xn_ref, w_ref, bd_ref, q_ref, k_ref, v_ref, buf_ref, *,
                    tm, n_prompt_tiles, prompt_seq_tiles, sample_seq_tiles):
    i = pl.program_id(0)
    seq_tiles = jnp.where(i < n_prompt_tiles, prompt_seq_tiles, sample_seq_tiles)
    pos = jnp.where(i < n_prompt_tiles, i, i - n_prompt_tiles) % seq_tiles
    buf_ref[0:8, :] = jnp.where(pos == 0, 0.0, xp_ref[...])
    buf_ref[8:8 + tm, :] = x_ref[...]
    buf_ref[8 + tm:16 + tm, :] = jnp.where(pos == seq_tiles - 1, 0.0, xn_ref[...])
    pad = (DN_CONV - 1) // 2
    acc = jnp.zeros((tm, DN_QKV), f32)
    for t in range(DN_CONV):
        acc = acc + buf_ref[pl.ds(8 - pad + t, tm), :] * w_ref[t:t + 1, :]
    act = _silu(acc)
    bd = bd_ref[...]
    hk = DN_HEADS * DN_DK
    q, k = act[:, :hk], act[:, hk:2 * hk]
    q_ref[...] = q * lax.rsqrt(_group_sum64(q * q, bd) + NORM_EPS) * (DN_DK ** -0.5)
    k_ref[...] = k * lax.rsqrt(_group_sum64(k * k, bd) + NORM_EPS)
    v_ref[...] = act[:, 2 * hk:]


def _dn_prep(x, w, bd, n_prompt_tiles, prompt_seq_tiles, sample_seq_tiles, tm):
    T = x.shape[0]
    nb8 = tm // 8
    o = (pl.BlockSpec((tm, 256), lambda i: (i, 0)), jax.ShapeDtypeStruct((T, 256), f32))
    return pl.pallas_call(
        functools.partial(_dn_prep_kernel, tm=tm, n_prompt_tiles=n_prompt_tiles,
                          prompt_seq_tiles=prompt_seq_tiles, sample_seq_tiles=sample_seq_tiles),
        grid=(T // tm,),
        in_specs=[pl.BlockSpec((tm, DN_QKV), lambda i: (i, 0)),
                  pl.BlockSpec((8, DN_QKV), lambda i: (jnp.maximum(i * nb8 - 1, 0), 0)),
                  pl.BlockSpec((8, DN_QKV), lambda i: (jnp.minimum((i + 1) * nb8, T // 8 - 1), 0)),
                  _resident((8, DN_QKV)), _resident((256, 256))],
        out_specs=[o[0]] * 3,
        out_shape=[o[1]] * 3,
        scratch_shapes=[pltpu.VMEM((tm + 16, DN_QKV), f32)],
        compiler_params=_cparams(("parallel",)),
        name="dn_prep",
    )(x, x, x, w, bd)


def _dn_prep_chunks(chains, bd, bdb):
    n = 2 * DN_CHUNK
    R = range(len(chains))
    rid = lax.broadcasted_iota(jnp.int32, (n, n), 0)
    lane = lax.broadcasted_iota(jnp.int32, (1, LANES), 1)
    low = lane < 64
    on = bd > 0.0

    def fold(x):
        return x + pltpu.roll(x, 64, 1)

    g_rows, beta_rows = [], []
    for (q, k, v, a_row, b_row, eal_row, dtb_row, mi, ms, um) in chains:
        xa = a_row + dtb_row
        g_rows.append(-eal_row * (jnp.maximum(xa, 0.0) + jnp.log1p(jnp.exp(-jnp.abs(xa)))))
        beta_rows.append(jax.nn.sigmoid(b_row))
    g8 = [jnp.broadcast_to(g, (8, n)) for g in g_rows]
    gc_rows = [_split3_dot(g8[c], chains[c][9])[0:1] for c in R]
    gsum_rows = [_split3_dot(g8[c], bdb)[0:1] for c in R]
    cols = [jnp.where(rid == 0, gc_rows[c], jnp.where(rid == 1, beta_rows[c], 0.0)).T for c in R]
    cg = [jnp.broadcast_to(cols[c][:, 0:1], (n, n)) for c in R]
    cb = [jnp.broadcast_to(cols[c][:, 1:2], (n, n)) for c in R]
    decay = [jnp.exp(jnp.where(chains[c][7] > 0.0, cg[c] - gc_rows[c], NEG_BIG)) for c in R]
    eg = [jnp.exp(cg[c]) for c in R]
    k4 = [jnp.tile(chains[c][1], (2, 1)) * bd for c in R]
    kb = [k4[c] * cb[c] for c in R]
    k4b = [k4[c].astype(bf16) for c in R]
    a_mat = [_dot_nt(kb[c].astype(bf16), k4b[c]) * decay[c] * chains[c][8] for c in R]
    x = [jnp.where(low, fold(jnp.tile(chains[c][2], (2, 1)) * bd * cb[c]), fold(kb[c] * eg[c])) for c in R]
    b_mat = [-a for a in a_mat]
    for t in range(6):
        bb = [b.astype(bf16) for b in b_mat]
        x = [x[c] + _dot(bb[c], x[c].astype(bf16)) for c in R]
        if t < 5:
            b_mat = [_dot(bb[c], bb[c]) for c in R]
    q4 = [jnp.tile(chains[c][0], (2, 1)) * bd for c in R]
    intra = [(_dot_nt(q4[c].astype(bf16), k4b[c]) * decay[c]).astype(bf16) for c in R]
    out = []
    for c in R:
        xr = pltpu.roll(x[c], 64, 1)
        u_bd = jnp.where(low, x[c], xr) * bd
        w_bd = (jnp.where(low, xr, x[c]) * bd).astype(bf16)
        ekd = jnp.exp(jnp.where(on, gsum_rows[c] - cg[c], 0.0))
        out.append((u_bd, w_bd, (q4[c] * eg[c]).astype(bf16), intra[c], (k4[c] * ekd).T.astype(bf16),
                    jnp.exp(gsum_rows[c])))
    return out


def _dn_chunk_kernel(qf_ref, kf_ref, vf_ref, gf_ref, qb_ref, kb_ref, vb_ref, gb_ref, prm_ref,
                     mif_ref, msf_ref, mib_ref, msb_ref, bd_ref, of_ref, ob_ref, sf_ref, sb_ref, *, nch):
    @pl.when(pl.program_id(1) == 0)
    def _():
        sf_ref[...] = jnp.zeros(sf_ref.shape, f32)
        sb_ref[...] = jnp.zeros(sb_ref.shape, f32)

    C = DN_CHUNK
    bd = bd_ref[...]
    bdb = bd.astype(bf16)
    mif, mib = mif_ref[...], mib_ref[...]
    umf, umb = mib.astype(bf16), mif.astype(bf16)
    msf, msb = msf_ref[...], msb_ref[...]
    chains = []
    for (q_ref, k_ref, v_ref, g_ref, ga, gb_, pa, mi, ms, um) in (
            (qf_ref, kf_ref, vf_ref, gf_ref, 0, 1, 0, mif, msf, umf),
            (qb_ref, kb_ref, vb_ref, gb_ref, 2, 3, 2, mib, msb, umb)):
        for c in range(nch):
            r = slice(c * C, (c + 1) * C)
            for p in range(2):
                l = slice(p * LANES, (p + 1) * LANES)
                chains.append((q_ref[r, l], k_ref[r, l], v_ref[r, l], g_ref[c, ga:ga + 1, l], g_ref[c, gb_:gb_ + 1, l],
                               prm_ref[pa:pa + 1, l], prm_ref[pa + 1:pa + 2, l], mi, ms, um))
    pre = _dn_prep_chunks(chains, bd, bdb)
    half = 2 * nch
    sts = [sf_ref[0], sf_ref[1], sb_ref[0], sb_ref[1]]
    J = range(4)
    for c in range(nch):
        cr = nch - 1 - c
        idx = [(c, of_ref, 0, 0), (c, of_ref, 0, 1), (cr, ob_ref, half, 0), (cr, ob_ref, half, 1)]
        prs = [pre[base + 2 * cc + p] for (cc, _, base, p) in idx]
        sbs = [sts[j].astype(bf16) for j in J]
        vn = [prs[j][0] - _dot(prs[j][1], sbs[j]) for j in J]
        vb = [vn[j].astype(bf16) for j in J]
        obd = [_dot(prs[j][2], sbs[j]) + _dot(prs[j][3], vb[j]) for j in J]
        sts = [sts[j] * prs[j][5] + _dot(prs[j][4], vb[j]) for j in J]
        for j, (cc, o_ref, base, p) in enumerate(idx):
            o_ref[cc * C:(cc + 1) * C, p * LANES:(p + 1) * LANES] = obd[j][0:64] + obd[j][64:128]
    sf_ref[0], sf_ref[1], sb_ref[0], sb_ref[1] = sts


def _dn_chunk(q, k, v, gates, prm, masks, *, B, S, off, nch=4):
    R = DN_CHUNK * nch
    N = S // R
    oc = off // R
    fwd = lambda b, c: (oc + b * N + c, 0)
    bwd = lambda b, c: (oc + b * N + N - 1 - c, 0)
    tokf = pl.BlockSpec((R, 256), fwd)
    tokb = pl.BlockSpec((R, 256), bwd)
    gf = pl.BlockSpec((nch, 8, 256), lambda b, c: (oc + b * N + c, 0, 0))
    gb = pl.BlockSpec((nch, 8, 256), lambda b, c: (oc + b * N + N - 1 - c, 0, 0))
    sq = _resident((LANES, LANES))
    return pl.pallas_call(
        functools.partial(_dn_chunk_kernel, nch=nch),
        grid=(B, N),
        in_specs=[tokf, tokf, tokf, gf, tokb, tokb, tokb, gb, _resident((8, 256)), sq, sq, sq, sq, sq],
        out_specs=[pl.BlockSpec((R, 256), lambda b, c: (b * N + c, 0)),
                   pl.BlockSpec((R, 256), lambda b, c: (b * N + N - 1 - c, 0))],
        out_shape=[jax.ShapeDtypeStruct((B * S, 256), f32)] * 2,
        scratch_shapes=[pltpu.VMEM((2, LANES, LANES), f32), pltpu.VMEM((2, LANES, LANES), f32)],
        compiler_params=_cparams(("parallel", "arbitrary")),
        name="dn_chunk",
    )(q, k, v, gates, q, k, v, gates, prm, *masks)


def _dil_kernel(q_ref, k_ref, v_ref, kp0, kn0, kp1, kn1, kp2, kn2, vp0, vn0, vp1, vn1, vp2, vn2,
                o_ref, os_ref, ls_ref, *, ts, nb, n_prompt_tiles, prompt_seq_tiles, sample_seq_tiles):
    i = pl.program_id(0)
    in_prompt = i < n_prompt_tiles
    seq_tiles = jnp.where(in_prompt, prompt_seq_tiles, sample_seq_tiles)
    pos = jnp.where(in_prompt, i, i - n_prompt_tiles) % seq_tiles
    first, last = pos == 0, pos == seq_tiles - 1
    halos = ((kp0, kn0, vp0, vn0), (kp1, kn1, vp1, vn1), (kp2, kn2, vp2, vn2))
    lane = lax.broadcasted_iota(jnp.int32, (1, LANES), 1)
    low = lane < DIL_HD
    R = DIL_RADIUS
    items = []
    for gi, (_, d) in enumerate(DIL_GROUPS):
        n = ts // d
        qb = min(LANES, n)
        for r in range(d):
            for j in range(n // qb):
                for pr in range(2):
                    items.append((gi, d, n, qb, r, j, pr))

    def load_keys(main, prv, nxt, gi, d, n, qb, r, t0, pr):
        parts = []
        lo, hi = t0 - R, t0 + qb + R
        if lo < 0:
            parts.append(prv[pr, pl.ds(r + d * (lo + R), -lo, stride=d), :])
        a, b = max(lo, 0), min(hi, n)
        parts.append(main[2 * gi + pr, pl.ds(r + d * a, b - a, stride=d), :])
        if hi > n:
            parts.append(nxt[pr, pl.ds(r, hi - n, stride=d), :])
        return jnp.concatenate(parts, axis=0).astype(bf16)

    masks = {}

    def mask_for(qb, n, t0):
        key = (qb, n, t0)
        if key not in masks:
            nkeys = qb + 2 * R
            row = lax.broadcasted_iota(jnp.int32, (qb, nkeys), 0)
            col = lax.broadcasted_iota(jnp.int32, (qb, nkeys), 1)
            valid = jnp.abs(col - R - row) <= R
            if t0 - R < 0:
                valid = valid & ((col >= R - t0) | jnp.logical_not(first))
            if t0 + qb + R > n:
                valid = valid & ((col < n + R - t0) | jnp.logical_not(last))
            masks[key] = valid
        return masks[key]

    for b0 in range(0, len(items), nb):
        batch = items[b0:b0 + nb]
        qs, ks, vs, vl = [], [], [], []
        for (gi, d, n, qb, r, j, pr) in batch:
            t0 = j * qb
            kp, kn, vp, vn = halos[gi]
            qs.append(q_ref[2 * gi + pr, pl.ds(r + d * t0, qb, stride=d), :].astype(bf16))
            ks.append(load_keys(k_ref, kp, kn, gi, d, n, qb, r, t0, pr))
            vs.append(load_keys(v_ref, vp, vn, gi, d, n, qb, r, t0, pr))
            vl.append(mask_for(qb, n, t0))
        U = [(i_, hh) for i_ in range(len(batch)) for hh in range(2)]
        qm = [jnp.where(low if hh == 0 else ~low, qs[i_], jnp.zeros_like(qs[i_])) for (i_, hh) in U]
        sc = [jnp.where(vl[i_], _dot_nt(qm[u], ks[i_]), NEG_BIG) for u, (i_, hh) in enumerate(U)]
        mm = [jnp.max(x, axis=1, keepdims=True) for x in sc]
        ee = [jnp.exp(sc[u] - mm[u]) for u in range(len(U))]
        dd = [jnp.sum(x, axis=1, keepdims=True) for x in ee]
        pv = [_dot((ee[u] / dd[u]).astype(bf16), vs[i_]) for u, (i_, hh) in enumerate(U)]
        ll = [mm[u] + jnp.log(dd[u]) for u in range(len(U))]
        for i_, (gi, d, n, qb, r, j, pr) in enumerate(batch):
            dst = pl.ds(r + d * j * qb, qb, stride=d)
            os_ref[2 * gi + pr, dst, :] = jnp.where(low, pv[2 * i_], pv[2 * i_ + 1])
            ls_ref[2 * gi + pr, dst, :] = jnp.where(low, ll[2 * i_], ll[2 * i_ + 1])
    for pr in range(2):
        l0, l1, l2 = ls_ref[pr], ls_ref[2 + pr], ls_ref[4 + pr]
        mx = jnp.maximum(jnp.maximum(l0, l1), l2)
        e0, e1, e2 = jnp.exp(l0 - mx), jnp.exp(l1 - mx), jnp.exp(l2 - mx)
        y = (e0 * os_ref[pr] + e1 * os_ref[2 + pr] + e2 * os_ref[4 + pr]) / (e0 + e1 + e2)
        o_ref[:, pr * LANES:(pr + 1) * LANES] = y.astype(o_ref.dtype)


def _dilated(q, k, v, n_prompt_tiles, prompt_seq_tiles, sample_seq_tiles, ts, nb=8):
    T = q.shape[1]
    main = pl.BlockSpec((6, ts, LANES), lambda i: (0, i, 0))
    halo_specs = []
    for gi, (_, d) in enumerate(DIL_GROUPS):
        h = DIL_RADIUS * d
        per = ts // h
        halo_specs.append(pl.BlockSpec((2, h, LANES),
                                       lambda i, per=per, gi=gi: (gi, jnp.maximum(i * per - 1, 0), 0)))
        halo_specs.append(pl.BlockSpec((2, h, LANES),
                                       lambda i, per=per, gi=gi, h=h: (gi, jnp.minimum((i + 1) * per, T // h - 1), 0)))
    return pl.pallas_call(
        functools.partial(_dil_kernel, ts=ts, nb=nb, n_prompt_tiles=n_prompt_tiles,
                          prompt_seq_tiles=prompt_seq_tiles, sample_seq_tiles=sample_seq_tiles),
        grid=(T // ts,),
        in_specs=[main, main, main] + halo_specs + halo_specs,
        out_specs=pl.BlockSpec((ts, 256), lambda i: (i, 0)),
        out_shape=jax.ShapeDtypeStruct((T, 256), bf16),
        scratch_shapes=[pltpu.VMEM((6, ts, LANES), f32), pltpu.VMEM((6, ts, LANES), f32)],
        compiler_params=_cparams(("parallel",)),
        name="dilated",
    )(q, k, v, *([k] * 6), *([v] * 6))


def _merge_kernel(x_ref, g_ref, ya_ref, yb_ref, of_ref, ob_ref, z_ref, dng_ref, bd_ref, yd_ref,
                  wg_ref, wb_ref, wo_ref, out_ref):
    x = x_ref[...]
    hb = _rms(x, g_ref[...]).astype(bf16)
    o = of_ref[...] + ob_ref[...]
    ms = _group_sum64(o * o, bd_ref[...]) * (1.0 / DN_DV)
    yc = o * lax.rsqrt(ms + NORM_EPS) * dng_ref[...] * _silu(z_ref[...].astype(f32))
    ys = (ya_ref[...], yb_ref[...], yc.astype(bf16), yd_ref[...])
    merged = jnp.zeros_like(x)
    for n in range(4):
        merged = merged + jax.nn.sigmoid(_dot(hb, wg_ref[n])) * _dot(ys[n], wb_ref[n])
    out_ref[...] = x + _dot(merged.astype(bf16), wo_ref[...])


def _merge(x, g, ya, yb, of, ob, z, dng, bd, yd, wg, wb, wo, tm=512):
    T = x.shape[0]
    t256 = pl.BlockSpec((tm, 256), lambda i: (i, 0))
    return pl.pallas_call(
        _merge_kernel,
        grid=(T // tm,),
        in_specs=[pl.BlockSpec((tm, D_MODEL), lambda i: (i, 0)), _resident((1, D_MODEL)),
                  t256, t256, t256, t256, t256, _resident((1, 256)), _resident((256, 256)), t256,
                  _resident((4, D_MODEL, D_MODEL)), _resident((4, 256, D_MODEL)), _resident((D_MODEL, D_MODEL))],
        out_specs=pl.BlockSpec((tm, D_MODEL), lambda i: (i, 0)),
        out_shape=jax.ShapeDtypeStruct((T, D_MODEL), f32),
        compiler_params=_cparams(("parallel",)),
        name="merge",
    )(x, g, ya, yb, of, ob, z, dng, bd, yd, wg, wb, wo)


def _rope_tables(smax):
    pos = jnp.arange(smax, dtype=f32)[:, None]

    def cs(d):
        inv = ROPE_THETA ** (-jnp.arange(0, d, 2, dtype=f32) / d)
        ang = pos * inv[None, :]
        return jnp.cos(ang), jnp.sin(ang)

    def head_pattern(d):
        c, s = cs(d)
        z = jnp.zeros_like(s)
        return jnp.concatenate([c, c], 1), jnp.concatenate([-s, z], 1), jnp.concatenate([z, s], 1)

    c32, a32, b32 = head_pattern(32)
    c64, a64, b64 = head_pattern(64)
    one = jnp.ones((smax, 64), f32)
    z64 = jnp.zeros((smax, 64), f32)
    z32 = jnp.zeros((smax, 32), f32)
    set_a = [jnp.tile(t, (1, 4)) for t in (c32, a32, b32)]
    set_b = [jnp.concatenate([one, c32, z32], 1), jnp.concatenate([z64, a32, z32], 1),
             jnp.concatenate([z64, b32, z32], 1)]
    set_c = [jnp.tile(t, (1, 2)) for t in (c64, a64, b64)]
    return jnp.stack(set_a + set_b + set_c, 0)


def _head_pad(w, n_heads, width, lo, hi, at=0):
    k = w.shape[0]
    w = w.reshape(k, n_heads, width)[:, :, lo:hi]
    out = jnp.zeros((k, n_heads, LANES), w.dtype).at[:, :, at:at + hi - lo].set(w)
    return out.reshape(k, n_heads * LANES)


def _layer_weights(w, li):
    sizes = (MLA_Q_RANK, MLA_KV_RANK, MLA_ROPE, 256, 256, 256, DN_QKV, 4, 4, 4, 4, 256,
             DIL_COLS, DIL_COLS, DIL_COLS)
    offs = np.cumsum((0,) + sizes)
    win = w['w_in'][li]
    col = lambda n: win[:, offs[n]:offs[n + 1]]
    kr_pad = jnp.zeros((D_MODEL, LANES), f32).at[:, MLA_NOPE:MLA_NOPE + MLA_ROPE].set(col(2))
    gates = jnp.concatenate([col(7), col(8), col(9), col(10), jnp.zeros((D_MODEL, LANES - 16), f32)], 1)
    big = jnp.concatenate([col(0), col(1), kr_pad, col(3), col(4), _head_pad(col(5), 4, 64, 0, 64),
                           col(6), gates, col(11), col(12), col(13), col(14)], 1).astype(bf16)
    assert big.shape[1] == C_END
    ukv = w['mla_w_ukv'][li]
    out = dict(
        w_in=big,
        wuq=_head_pad(w['mla_w_uq'][li], 4, MLA_NOPE + MLA_ROPE, 0, MLA_NOPE + MLA_ROPE).astype(bf16),
        wuk=_head_pad(ukv, 4, MLA_NOPE + MLA_V, 0, MLA_NOPE).astype(bf16),
        wuv=_head_pad(ukv, 4, MLA_NOPE + MLA_V, MLA_NOPE, MLA_NOPE + MLA_V).astype(bf16),
    )
    for n in ('ff1_w1', 'ff1_w3', 'ff1_w2', 'ff2_w1', 'ff2_w3', 'ff2_w2', 'w_gate', 'w_branch', 'w_out',
              'ple_gate', 'ple_proj'):
        out[n] = w[n][li].astype(bf16)
    return out


def _dn_masks(n_heads):
    idx = np.arange(n_heads * DN_CHUNK)
    same = (idx[:, None] // DN_CHUNK) == (idx[None, :] // DN_CHUNK)
    i, j = idx[:, None] % DN_CHUNK, idx[None, :] % DN_CHUNK
    mk = lambda m: jnp.asarray((same & m).astype(np.float32))
    return mk(i >= j), mk(i > j), mk(i <= j), mk(i < j), mk(np.ones_like(same))


def _seq_tile(s, pref):
    return min(s, pref)


def kernel(x_prompt, x_sample, p_prompt, p_sample, norm_ff1, ff1_w1, ff1_w3, ff1_w2, norm_mix, w_in, mla_q_norm, mla_kv_norm, mla_w_uq, mla_w_ukv, diff_lambda, diff_subln, dn_conv, dn_a_log, dn_dt_bias, dn_out_norm, w_branch, w_gate, w_out, norm_ff2, ff2_w1, ff2_w3, ff2_w2, norm_ple, ple_gate, ple_proj, norm_final):
    w = dict(norm_ff1=norm_ff1, ff1_w1=ff1_w1, ff1_w3=ff1_w3, ff1_w2=ff1_w2, norm_mix=norm_mix, w_in=w_in,
             mla_q_norm=mla_q_norm, mla_kv_norm=mla_kv_norm, mla_w_uq=mla_w_uq, mla_w_ukv=mla_w_ukv,
             diff_lambda=diff_lambda, diff_subln=diff_subln, dn_conv=dn_conv, dn_a_log=dn_a_log,
             dn_dt_bias=dn_dt_bias, dn_out_norm=dn_out_norm, w_branch=w_branch, w_gate=w_gate, w_out=w_out,
             norm_ff2=norm_ff2, ff2_w1=ff2_w1, ff2_w3=ff2_w3, ff2_w2=ff2_w2, norm_ple=norm_ple,
             ple_gate=ple_gate, ple_proj=ple_proj)
    depth = w_in.shape[0]
    BP, SP, _ = x_prompt.shape
    BS, SS, _ = x_sample.shape
    TP, TS = BP * SP, BS * SS
    T = TP + TS
    groups = ((BP, SP, 0), (BS, SS, TP))
    tm = min(512, SP, SS)
    assert TP % tm == 0 and TS % tm == 0 and SP % tm == 0 and SS % tm == 0

    xs = [x_prompt.reshape(TP, D_MODEL), x_sample.reshape(TS, D_MODEL)]
    ps = [p_prompt.reshape(depth, TP, PLE_DIM), p_sample.reshape(depth, TS, PLE_DIM)]
    npt = TP // tm

    tab = _rope_tables(max(SP, SS))
    masks = _dn_masks(2)
    bd_bf = _dn_masks(DN_HEADS)[4].astype(bf16)
    ones_pat = jnp.tile(jnp.concatenate([jnp.zeros((1, 64), f32), jnp.ones((1, 64), f32)], 1), (1, 4))
    row = lambda v: v.reshape(1, -1).astype(f32)
    mla_units = tuple((h, h, h) for h in range(MLA_HEADS))
    diff_units = tuple((hm, hm // 4, hm // 2) for hm in range(2 * DIFF_HEADS))
    no_sc = jnp.zeros((2,), f32)
    no_g = jnp.zeros((1, LANES), f32)

    for li in range(depth):
        lw = _layer_weights(w, li)
        x = _ffn(xs, row(norm_ff1[li]), lw['ff1_w1'], lw['ff1_w3'], lw['ff1_w2'], n_prompt_tiles=npt, tm=tm)

        (mq, mk, mv, dq, dk, dv, nqkv, ng, nz, lq, lk, lv) = _inproj(
            x, row(norm_mix[li]), lw['w_in'], row(mla_q_norm[li]), row(mla_kv_norm[li]),
            lw['wuq'], lw['wuk'], lw['wuv'], ones_pat, tab, TP // tm, SP // tm, SS // tm, tm)

        lf = diff_lambda[li].astype(f32)
        lambda_init = 0.8 - 0.6 * math.exp(-0.3 * li)
        lam = jnp.exp(jnp.sum(lf[0] * lf[1])) - jnp.exp(jnp.sum(lf[2] * lf[3])) + lambda_init
        sc = jnp.stack([lam, jnp.asarray(1.0 - lambda_init, f32)]).astype(f32)
        subln = jnp.tile(row(diff_subln[li]), (1, 2))
        ya, yb = [], []
        for (B, S, off) in groups:
            tq = _seq_tile(S, 2048)
            ya.append(_flash(mq, mk, mv, no_sc, no_g, B=B, S=S, off=off, units=mla_units, diff=False,
                             tq=tq, tk=_seq_tile(S, 1024), name="mla_attn"))
            yb.append(_flash(dq, dk, dv, sc, subln, B=B, S=S, off=off, units=diff_units, diff=True,
                             tq=tq, tk=_seq_tile(S, 256), name="diff_attn"))
        ya, yb = jnp.concatenate(ya, 0), jnp.concatenate(yb, 0)

        conv_w = jnp.concatenate([dn_conv[li].astype(f32), jnp.zeros((8 - DN_CONV, DN_QKV), f32)], 0)
        nq, nk, nv = _dn_prep(nqkv, conv_w, bd_bf, TP // tm, SP // tm, SS // tm, tm)
        gates = ng[:, :16].reshape(T // DN_CHUNK, DN_CHUNK, 4, DN_HEADS)
        gates = jnp.transpose(gates, (0, 2, 3, 1)).reshape(T // DN_CHUNK, 4, 4 * DN_CHUNK)
        gates = jnp.concatenate([gates, jnp.zeros_like(gates)], 1)
        al, dtb = dn_a_log[li].astype(f32), dn_dt_bias[li].astype(f32)
        rep = lambda v: jnp.repeat(v, DN_CHUNK)[None, :]
        prm = jnp.concatenate([rep(jnp.exp(al[0])), rep(dtb[0]), rep(jnp.exp(al[1])), rep(dtb[1]),
                               jnp.zeros((4, 4 * DN_CHUNK), f32)], 0)
        of, ob = [], []
        for (B, S, off) in groups:
            f_, b_ = _dn_chunk(nq, nk, nv, gates, prm, masks, B=B, S=S, off=off)
            of.append(f_)
            ob.append(b_)
        of, ob = jnp.concatenate(of, 0), jnp.concatenate(ob, 0)

        ts = min(1024, SP, SS)
        yd = _dilated(lq, lk, lv, TP // ts, SP // ts, SS // ts, ts)

        x = _merge(x, row(norm_mix[li]), ya, yb, of, ob, nz, jnp.tile(row(dn_out_norm[li]), (1, 4)), bd_bf,
                   yd, lw['w_gate'], lw['w_branch'], lw['w_out'], tm=tm)
        last = li == depth - 1
        ple = (ps, li, row(norm_ple[li]), lw['ple_gate'], lw['ple_proj'], row(norm_final))
        x = _ffn([x], row(norm_ff2[li]), lw['ff2_w1'], lw['ff2_w3'], lw['ff2_w2'], n_prompt_tiles=npt, tm=tm,
                 ple=ple, final=last, split_out=last)
        xs = [x]

    return (x[0].reshape(BP, SP, D_MODEL), x[1].reshape(BS, SS, D_MODEL))
```

```python
import functools
import math

import numpy as np
import jax
import jax.numpy as jnp
from jax import lax
from jax.experimental import pallas as pl
from jax.experimental.pallas import tpu as pltpu

f32 = jnp.float32
bf16 = jnp.bfloat16

D_MODEL = 1024
PLE_DIM = 256
D_FF = 2816
ROPE_THETA = 10000.0
NORM_EPS = 1e-6
NEG_BIG = -1e30
LOG2E = math.log2(math.e)

MLA_HEADS = 4
MLA_Q_RANK = 256
MLA_KV_RANK = 128
MLA_NOPE = 64
MLA_ROPE = 32
MLA_V = 64

DIFF_HEADS = 4
DIFF_HD = 32
DIFF_VD = 64

DN_HEADS = 4
DN_DK = 64
DN_DV = 64
DN_CONV = 5
DN_CHUNK = 64
DN_QKV = DN_HEADS * (2 * DN_DK + DN_DV)

DIL_GROUPS = ((128, 1), (512, 4), (2048, 16))
DIL_HEADS = 4
DIL_HD = 64
DIL_RADIUS = 64
DIL_COLS = len(DIL_GROUPS) * DIL_HEADS * DIL_HD

LANES = 128
VMEM_LIMIT = 56 * 1024 * 1024

C_CQKV = 0
C_BQ = 512
C_BK = 768
C_BV = 1024
C_DNQKV = 1536
C_DNG = 2304
C_DNZ = 2432
C_DQ = 2688
C_DK = 3456
C_DV = 4224
C_END = 4992


def _cparams(sem):
    return pltpu.CompilerParams(dimension_semantics=sem, vmem_limit_bytes=VMEM_LIMIT)


def _resident(shape):
    nd = len(shape)
    return pl.BlockSpec(shape, lambda *_: (0,) * nd, pipeline_mode=pl.Buffered(1))


def _dot(a, b):
    return jnp.dot(a, b, preferred_element_type=f32)


def _dot_nt(a, b):
    return lax.dot_general(a, b, (((1,), (1,)), ((), ())), preferred_element_type=f32)


def _rms(x, g):
    ms = jnp.mean(x * x, axis=-1, keepdims=True)
    return x * lax.rsqrt(ms + NORM_EPS) * g


def _silu(x):
    return x * jax.nn.sigmoid(x)


def _split3_dot(x, m):
    x1 = x.astype(bf16)
    r1 = x - x1.astype(f32)
    x2 = r1.astype(bf16)
    x3 = (r1 - x2.astype(f32)).astype(bf16)
    return _dot(x1, m) + _dot(x2, m) + _dot(x3, m)


def _group_sum64(x, bd):
    return _split3_dot(x, bd)


def _rope128(x, c, s1, s2, half):
    return x * c + pltpu.roll(x, LANES - half, 1) * s1 + pltpu.roll(x, half, 1) * s2


def _ffn_kernel(*refs, fc, n_in, ple, final, n_out, n_prompt_tiles):
    refs = list(refs)
    take = lambda k: [refs.pop(0) for _ in range(k)]
    x_refs = take(n_in)
    g_ref, w1_ref, w3_ref, w2_ref = take(4)
    if ple:
        p_refs = take(2)
        gp_ref, wg_ref, wp_ref, gf_ref = take(4)
    o_refs = take(n_out)
    in_prompt = pl.program_id(0) < n_prompt_tiles

    def pick(rs):
        return rs[0][...] if len(rs) == 1 else jnp.where(in_prompt, rs[0][...], rs[1][...])

    x = pick(x_refs)
    xn = _rms(x, g_ref[...]).astype(bf16)
    y = jnp.zeros_like(x)
    for c in range(D_FF // fc):
        a = _dot(xn, w1_ref[:, c * fc:(c + 1) * fc])
        b = _dot(xn, w3_ref[:, c * fc:(c + 1) * fc])
        y = y + _dot((_silu(a) * b).astype(bf16), w2_ref[c * fc:(c + 1) * fc, :])
    y = x + 0.5 * y
    if ple:
        yn = _rms(y, gp_ref[...]).astype(bf16)
        gate = jax.nn.sigmoid(_dot(yn, wg_ref[...]))
        y = y + gate * _dot(pick(p_refs).astype(bf16), wp_ref[...])
        if final:
            y = _rms(y, gf_ref[...])
    if n_out == 1:
        o_refs[0][...] = y
    else:
        @pl.when(in_prompt)
        def _():
            o_refs[0][...] = y

        @pl.when(jnp.logical_not(in_prompt))
        def _():
            o_refs[1][...] = y


def _ffn(xs, g, w1, w3, w2, *, n_prompt_tiles, tm, ple=None, final=False, split_out=False, fc=256):
    n_in = len(xs)
    tp = n_prompt_tiles * tm
    T = sum(x.shape[0] for x in xs)

    def tok_specs(width, dual, lead=None):
        def mk(fn):
            if lead is None:
                return pl.BlockSpec((tm, width), lambda i: (fn(i), 0))
            return pl.BlockSpec((None, tm, width), lambda i: (lead, fn(i), 0))
        if not dual:
            return [mk(lambda i: i)]
        return [mk(lambda i: jnp.minimum(i, n_prompt_tiles - 1)), mk(lambda i: jnp.maximum(i - n_prompt_tiles, 0))]

    args = list(xs) + [g, w1, w3, w2]
    in_specs = tok_specs(D_MODEL, n_in == 2) + [_resident((1, D_MODEL)), _resident((D_MODEL, D_FF)),
                                                _resident((D_MODEL, D_FF)), _resident((D_FF, D_MODEL))]
    if ple is not None:
        ps, li, gp, wg, wp, gf = ple
        args += list(ps) + [gp, wg, wp, gf]
        in_specs += tok_specs(PLE_DIM, True, lead=li) + [_resident((1, D_MODEL)), _resident((D_MODEL, D_MODEL)),
                                                         _resident((PLE_DIM, D_MODEL)), _resident((1, D_MODEL))]
    if split_out:
        out_specs = [pl.BlockSpec((tm, D_MODEL), lambda i: (jnp.minimum(i, n_prompt_tiles - 1), 0)),
                     pl.BlockSpec((tm, D_MODEL), lambda i: (jnp.maximum(i - n_prompt_tiles, 0), 0))]
        out_shape = [jax.ShapeDtypeStruct((tp, D_MODEL), f32), jax.ShapeDtypeStruct((T - tp, D_MODEL), f32)]
    else:
        out_specs = pl.BlockSpec((tm, D_MODEL), lambda i: (i, 0))
        out_shape = jax.ShapeDtypeStruct((T, D_MODEL), f32)
    return pl.pallas_call(
        functools.partial(_ffn_kernel, fc=fc, n_in=n_in, ple=ple is not None, final=final,
                          n_out=2 if split_out else 1, n_prompt_tiles=n_prompt_tiles),
        grid=(T // tm,),
        in_specs=in_specs,
        out_specs=out_specs,
        out_shape=out_shape,
        compiler_params=_cparams(("arbitrary",) if split_out else ("parallel",)),
        name="ffn_ple" if ple is not None else "ffn",
    )(*args)


def _inproj_kernel(x_ref, g_ref, w_ref, qn_ref, kvn_ref, wuq_ref, wuk_ref, wuv_ref, ones_ref, tab_ref,
                   mq_ref, mk_ref, mv_ref, dq_ref, dk_ref, dv_ref, nqkv_ref, ng_ref, nz_ref,
                   lq_ref, lk_ref, lv_ref):
    hb = _rms(x_ref[...], g_ref[...]).astype(bf16)

    def proj(lo, hi):
        return _dot(hb, w_ref[:, lo:hi])

    def tabs(s):
        return tab_ref[3 * s], tab_ref[3 * s + 1], tab_ref[3 * s + 2]

    ones_pat = ones_ref[...]

    c = proj(C_CQKV, C_CQKV + 512)
    cq, ckv, krp = c[:, :MLA_Q_RANK], c[:, MLA_Q_RANK:MLA_Q_RANK + MLA_KV_RANK], c[:, 384:512]
    q = _dot(_rms(cq, qn_ref[...]).astype(bf16), wuq_ref[...])
    kvn = _rms(ckv, kvn_ref[...]).astype(bf16)
    kn = _dot(kvn, wuk_ref[...])
    cb, s1b, s2b = tabs(1)
    q_scale = (MLA_NOPE + MLA_ROPE) ** -0.5 * LOG2E
    kr = _rope128(krp, cb, s1b, s2b, MLA_ROPE // 2)
    for h in range(MLA_HEADS):
        sl = slice(h * LANES, (h + 1) * LANES)
        mq_ref[:, sl] = (_rope128(q[:, sl], cb, s1b, s2b, MLA_ROPE // 2) * q_scale).astype(bf16)
        mk_ref[:, sl] = (kn[:, sl] + kr).astype(bf16)
    mv_ref[...] = (_dot(kvn, wuv_ref[...]) + ones_pat).astype(bf16)

    ca, s1a, s2a = tabs(0)
    lane = lax.broadcasted_iota(jnp.int32, (1, LANES), 1)
    bq = proj(C_BQ, C_BQ + 256)
    bk = proj(C_BK, C_BK + 256)
    for ch in range(2):
        sl = slice(ch * LANES, (ch + 1) * LANES)
        qr = _rope128(bq[:, sl], ca, s1a, s2a, DIFF_HD // 2) * (DIFF_HD ** -0.5 * LOG2E)
        for slot in range(4):
            hm = ch * 4 + slot
            dq_ref[:, hm * LANES:(hm + 1) * LANES] = jnp.where(lane // DIFF_HD == slot, qr, 0.0).astype(bf16)
        dk_ref[:, sl] = _rope128(bk[:, sl], ca, s1a, s2a, DIFF_HD // 2).astype(bf16)
    dv_ref[...] = (proj(C_BV, C_BV + 512) + ones_pat).astype(bf16)

    nqkv_ref[...] = proj(C_DNQKV, C_DNQKV + DN_QKV)
    ng_ref[...] = proj(C_DNG, C_DNG + LANES)
    nz_ref[...] = proj(C_DNZ, C_DNZ + 256).astype(bf16)

    cc, s1c, s2c = tabs(2)
    lq = proj(C_DQ, C_DQ + DIL_COLS)
    lk = proj(C_DK, C_DK + DIL_COLS)
    lv = proj(C_DV, C_DV + DIL_COLS)
    for ch in range(DIL_COLS // LANES):
        sl = slice(ch * LANES, (ch + 1) * LANES)
        lq_ref[ch] = _rope128(lq[:, sl], cc, s1c, s2c, DIL_HD // 2) * (DIL_HD ** -0.5)
        lk_ref[ch] = _rope128(lk[:, sl], cc, s1c, s2c, DIL_HD // 2)
        lv_ref[ch] = lv[:, sl]


def _inproj(x, g, w, qn, kvn, wuq, wuk, wuv, ones_pat, tab, n_prompt_tiles, prompt_pos_tiles, sample_pos_tiles,
            tm):
    T = x.shape[0]

    def pos_map(i):
        return (0, jnp.where(i < n_prompt_tiles, i % prompt_pos_tiles, (i - n_prompt_tiles) % sample_pos_tiles), 0)

    def tok(w_, dt):
        return pl.BlockSpec((tm, w_), lambda i: (i, 0)), jax.ShapeDtypeStruct((T, w_), dt)

    dil = (pl.BlockSpec((DIL_COLS // LANES, tm, LANES), lambda i: (0, i, 0)),
           jax.ShapeDtypeStruct((DIL_COLS // LANES, T, LANES), f32))
    outs = [tok(512, bf16), tok(512, bf16), tok(512, bf16),
            tok(1024, bf16), tok(256, bf16), tok(512, bf16),
            tok(DN_QKV, f32), tok(LANES, f32), tok(256, bf16),
            dil, dil, dil]
    return pl.pallas_call(
        _inproj_kernel,
        grid=(T // tm,),
        in_specs=[pl.BlockSpec((tm, D_MODEL), lambda i: (i, 0)),
                  _resident((1, D_MODEL)), _resident((D_MODEL, C_END)),
                  _resident((1, MLA_Q_RANK)), _resident((1, MLA_KV_RANK)),
                  _resident((MLA_Q_RANK, 512)), _resident((MLA_KV_RANK, 512)), _resident((MLA_KV_RANK, 512)),
                  _resident((1, 512)),
                  pl.BlockSpec((9, tm, LANES), pos_map)],
        out_specs=[o[0] for o in outs],
        out_shape=[o[1] for o in outs],
        compiler_params=_cparams(("parallel",)),
        name="inproj",
    )(x, g, w, qn, kvn, wuq, wuk, wuv, ones_pat, tab)


def _flash_kernel(sc_ref, g_ref, q_ref, k_ref, v_ref, o_ref, m_ref, acc_ref, *, units, diff, nk):
    ki = pl.program_id(2)

    @pl.when(ki == 0)
    def _():
        m_ref[...] = jnp.full(m_ref.shape, NEG_BIG, f32)
        acc_ref[...] = jnp.zeros(acc_ref.shape, f32)

    nc = k_ref.shape[0] // LANES
    for u, (qi, kc, vh) in enumerate(units):
        q = q_ref[:, qi * LANES:(qi + 1) * LANES]
        k = k_ref[:, kc * LANES:(kc + 1) * LANES]
        s = _dot_nt(q, k)
        m_prev = m_ref[u]
        mx = s[:, 0:LANES]
        for c in range(1, nc):
            mx = jnp.maximum(mx, s[:, c * LANES:(c + 1) * LANES])
        m_new = jnp.maximum(m_prev, jnp.max(mx, axis=1, keepdims=True))
        alpha = jnp.exp2(m_prev - m_new)
        p = jnp.concatenate([jnp.exp2(s[:, c * LANES:(c + 1) * LANES] - m_new).astype(bf16)
                             for c in range(nc)], axis=1)
        acc_ref[u] = alpha * acc_ref[u] + _dot(p, v_ref[:, vh * LANES:(vh + 1) * LANES])
        m_ref[u] = m_new

    @pl.when(ki == nk - 1)
    def _():
        lane = lax.broadcasted_iota(jnp.int32, (1, LANES), 1)
        low = lane < 64

        def normed(u):
            a = acc_ref[u]
            return a / pltpu.roll(a, 64, 1)

        heads = []
        if diff:
            lam, post = sc_ref[0], sc_ref[1]
            for h in range(DIFF_HEADS):
                o = normed(2 * h) - lam * normed(2 * h + 1)
                ms = jnp.sum(jnp.where(low, o * o, 0.0), axis=1, keepdims=True) * (1.0 / DIFF_VD)
                heads.append(o * lax.rsqrt(ms + NORM_EPS) * g_ref[...] * post)
        else:
            heads = [normed(u) for u in range(len(units))]
        for pr in range(2):
            pair = jnp.where(low, heads[2 * pr], pltpu.roll(heads[2 * pr + 1], 64, 1))
            o_ref[:, pr * LANES:(pr + 1) * LANES] = pair.astype(o_ref.dtype)


def _flash(q, k, v, sc, g, *, B, S, off, units, diff, tq, tk, name):
    nq, nk = S // tq, S // tk
    oq, ok = off // tq, off // tk
    n_acc = len(units)
    return pl.pallas_call(
        functools.partial(_flash_kernel, units=units, diff=diff, nk=nk),
        grid=(B, nq, nk),
        in_specs=[pl.BlockSpec(memory_space=pltpu.SMEM),
                  pl.BlockSpec((1, LANES), lambda b, i, j: (0, 0)),
                  pl.BlockSpec((tq, q.shape[1]), lambda b, i, j: (oq + b * nq + i, 0)),
                  pl.BlockSpec((tk, k.shape[1]), lambda b, i, j: (ok + b * nk + j, 0)),
                  pl.BlockSpec((tk, v.shape[1]), lambda b, i, j: (ok + b * nk + j, 0))],
        out_specs=pl.BlockSpec((tq, 256), lambda b, i, j: (b * nq + i, 0)),
        out_shape=jax.ShapeDtypeStruct((B * S, 256), bf16),
        scratch_shapes=[pltpu.VMEM((n_acc, tq, LANES), f32), pltpu.VMEM((n_acc, tq, LANES), f32)],
        compiler_params=_cparams(("parallel", "parallel", "arbitrary")),
        name=name,
    )(sc, g, q, k, v)


def _dn_prep_kernel(x_ref, xp_ref, xn_ref, w_ref, bd_ref, q_ref, k_ref, v_ref, buf_ref, *,
                    tm, n_prompt_tiles, prompt_seq_tiles, sample_seq_tiles):
    i = pl.program_id(0)
    seq_tiles = jnp.where(i < n_prompt_tiles, prompt_seq_tiles, sample_seq_tiles)
    pos = jnp.where(i < n_prompt_tiles, i, i - n_prompt_tiles) % seq_tiles
    buf_ref[0:8, :] = jnp.where(pos == 0, 0.0, xp_ref[...])
    buf_ref[8:8 + tm, :] = x_ref[...]
    buf_ref[8 + tm:16 + tm, :] = jnp.where(pos == seq_tiles - 1, 0.0, xn_ref[...])
    pad = (DN_CONV - 1) // 2
    acc = jnp.zeros((tm, DN_QKV), f32)
    for t in range(DN_CONV):
        acc = acc + buf_ref[pl.ds(8 - pad + t, tm), :] * w_ref[t:t + 1, :]
    act = _silu(acc)
    bd = bd_ref[...]
    hk = DN_HEADS * DN_DK
    q, k = act[:, :hk], act[:, hk:2 * hk]
    q_ref[...] = q * lax.rsqrt(_group_sum64(q * q, bd) + NORM_EPS) * (DN_DK ** -0.5)
    k_ref[...] = k * lax.rsqrt(_group_sum64(k * k, bd) + NORM_EPS)
    v_ref[...] = act[:, 2 * hk:]


def _dn_prep(x, w, bd, n_prompt_tiles, prompt_seq_tiles, sample_seq_tiles, tm):
    T = x.shape[0]
    nb8 = tm // 8
    o = (pl.BlockSpec((tm, 256), lambda i: (i, 0)), jax.ShapeDtypeStruct((T, 256), f32))
    return pl.pallas_call(
        functools.partial(_dn_prep_kernel, tm=tm, n_prompt_tiles=n_prompt_tiles,
                          prompt_seq_tiles=prompt_seq_tiles, sample_seq_tiles=sample_seq_tiles),
        grid=(T // tm,),
        in_specs=[pl.BlockSpec((tm, DN_QKV), lambda i: (i, 0)),
                  pl.BlockSpec((8, DN_QKV), lambda i: (jnp.maximum(i * nb8 - 1, 0), 0)),
                  pl.BlockSpec((8, DN_QKV), lambda i: (jnp.minimum((i + 1) * nb8, T // 8 - 1), 0)),
                  _resident((8, DN_QKV)), _resident((256, 256))],
        out_specs=[o[0]] * 3,
        out_shape=[o[1]] * 3,
        scratch_shapes=[pltpu.VMEM((tm + 16, DN_QKV), f32)],
        compiler_params=_cparams(("parallel",)),
        name="dn_prep",
    )(x, x, x, w, bd)


def _dn_prep_chunks(chains, bd, bdb, blocks):
    n = 2 * DN_CHUNK
    R = range(len(chains))
    rid = lax.broadcasted_iota(jnp.int32, (n, n), 0)
    lane = lax.broadcasted_iota(jnp.int32, (1, LANES), 1)
    low = lane < 64
    on = bd > 0.0

    def fold(x):
        return x + pltpu.roll(x, 64, 1)

    g_rows, beta_rows = [], []
    for (q, k, v, a_row, b_row, eal_row, dtb_row, mi, ms, um) in chains:
        xa = a_row + dtb_row
        g_rows.append(-eal_row * (jnp.maximum(xa, 0.0) + jnp.log1p(jnp.exp(-jnp.abs(xa)))))
        beta_rows.append(jax.nn.sigmoid(b_row))
    g8 = [jnp.broadcast_to(g, (8, n)) for g in g_rows]
    gc_rows = [_split3_dot(g8[c], chains[c][9])[0:1] for c in R]
    gsum_rows = [_split3_dot(g8[c], bdb)[0:1] for c in R]
    cols = [jnp.where(rid == 0, gc_rows[c], jnp.where(rid == 1, beta_rows[c], 0.0)).T for c in R]
    cg = [jnp.broadcast_to(cols[c][:, 0:1], (n, n)) for c in R]
    cb = [jnp.broadcast_to(cols[c][:, 1:2], (n, n)) for c in R]
    decay = [jnp.exp(jnp.where(chains[c][7] > 0.0, cg[c] - gc_rows[c], NEG_BIG)) for c in R]
    eg = [jnp.exp(cg[c]) for c in R]
    k4 = [jnp.tile(chains[c][1], (2, 1)) * bd for c in R]
    kb = [k4[c] * cb[c] for c in R]
    k4b = [k4[c].astype(bf16) for c in R]
    a_mat = [_dot_nt(kb[c].astype(bf16), k4b[c]) * decay[c] * chains[c][8] for c in R]
    rhs = [jnp.where(low, fold(jnp.tile(chains[c][2], (2, 1)) * bd * cb[c]), fold(kb[c] * eg[c])) for c in R]
    s16, m32, m64 = blocks
    eye = (rid == lax.broadcasted_iota(jnp.int32, (n, n), 1)).astype(f32)
    pw = [-(a_mat[c] * s16) for c in R]
    dm = [eye + pw[c] for c in R]
    for t in range(3):
        pb = [p.astype(bf16) for p in pw]
        pw = [_dot(pb[c], pb[c]) for c in R]
        dm = [dm[c] + _dot(pw[c].astype(bf16), dm[c].astype(bf16)) for c in R]
    for msk in (m32, m64):
        db = [d.astype(bf16) for d in dm]
        off = [_dot((a_mat[c] * msk).astype(bf16), db[c]) for c in R]
        dm = [dm[c] - _dot(db[c], off[c].astype(bf16)) for c in R]
    x = [_dot(dm[c].astype(bf16), rhs[c].astype(bf16)) for c in R]
    q4 = [jnp.tile(chains[c][0], (2, 1)) * bd for c in R]
    intra = [(_dot_nt(q4[c].astype(bf16), k4b[c]) * decay[c]).astype(bf16) for c in R]
    out = []
    for c in R:
        xr = pltpu.roll(x[c], 64, 1)
        u_bd = jnp.where(low, x[c], xr) * bd
        w_bd = (jnp.where(low, xr, x[c]) * bd).astype(bf16)
        ekd = jnp.exp(jnp.where(on, gsum_rows[c] - cg[c], 0.0))
        out.append((u_bd, w_bd, (q4[c] * eg[c]).astype(bf16), intra[c], (k4[c] * ekd).T.astype(bf16),
                    jnp.exp(gsum_rows[c])))
    return out


def _dn_chunk_kernel(qf_ref, kf_ref, vf_ref, gf_ref, qb_ref, kb_ref, vb_ref, gb_ref, prm_ref,
                     mif_ref, msf_ref, mib_ref, msb_ref, bd_ref, s16_ref, m32_ref, m64_ref,
                     of_ref, ob_ref, sf_ref, sb_ref, *, nch):
    @pl.when(pl.program_id(1) == 0)
    def _():
        sf_ref[...] = jnp.zeros(sf_ref.shape, f32)
        sb_ref[...] = jnp.zeros(sb_ref.shape, f32)

    C = DN_CHUNK
    bd = bd_ref[...]
    bdb = bd.astype(bf16)
    mif, mib = mif_ref[...], mib_ref[...]
    umf, umb = mib.astype(bf16), mif.astype(bf16)
    msf, msb = msf_ref[...], msb_ref[...]
    chains = []
    for (q_ref, k_ref, v_ref, g_ref, ga, gb_, pa, mi, ms, um) in (
            (qf_ref, kf_ref, vf_ref, gf_ref, 0, 1, 0, mif, msf, umf),
            (qb_ref, kb_ref, vb_ref, gb_ref, 2, 3, 2, mib, msb, umb)):
        for c in range(nch):
            r = slice(c * C, (c + 1) * C)
            for p in range(2):
                l = slice(p * LANES, (p + 1) * LANES)
                chains.append((q_ref[r, l], k_ref[r, l], v_ref[r, l], g_ref[c, ga:ga + 1, l], g_ref[c, gb_:gb_ + 1, l],
                               prm_ref[pa:pa + 1, l], prm_ref[pa + 1:pa + 2, l], mi, ms, um))
    pre = _dn_prep_chunks(chains, bd, bdb, (s16_ref[...], m32_ref[...], m64_ref[...]))
    half = 2 * nch
    sts = [sf_ref[0], sf_ref[1], sb_ref[0], sb_ref[1]]
    J = range(4)
    for c in range(nch):
        cr = nch - 1 - c
        idx = [(c, of_ref, 0, 0), (c, of_ref, 0, 1), (cr, ob_ref, half, 0), (cr, ob_ref, half, 1)]
        prs = [pre[base + 2 * cc + p] for (cc, _, base, p) in idx]
        sbs = [sts[j].astype(bf16) for j in J]
        vn = [prs[j][0] - _dot(prs[j][1], sbs[j]) for j in J]
        vb = [vn[j].astype(bf16) for j in J]
        obd = [_dot(prs[j][2], sbs[j]) + _dot(prs[j][3], vb[j]) for j in J]
        sts = [sts[j] * prs[j][5] + _dot(prs[j][4], vb[j]) for j in J]
        for j, (cc, o_ref, base, p) in enumerate(idx):
            o_ref[cc * C:(cc + 1) * C, p * LANES:(p + 1) * LANES] = obd[j][0:64] + obd[j][64:128]
    sf_ref[0], sf_ref[1], sb_ref[0], sb_ref[1] = sts


def _dn_chunk(q, k, v, gates, prm, masks, *, B, S, off, nch=4):
    R = DN_CHUNK * nch
    N = S // R
    oc = off // R
    fwd = lambda b, c: (oc + b * N + c, 0)
    bwd = lambda b, c: (oc + b * N + N - 1 - c, 0)
    tokf = pl.BlockSpec((R, 256), fwd)
    tokb = pl.BlockSpec((R, 256), bwd)
    gf = pl.BlockSpec((nch, 8, 256), lambda b, c: (oc + b * N + c, 0, 0))
    gb = pl.BlockSpec((nch, 8, 256), lambda b, c: (oc + b * N + N - 1 - c, 0, 0))
    sq = _resident((LANES, LANES))
    return pl.pallas_call(
        functools.partial(_dn_chunk_kernel, nch=nch),
        grid=(B, N),
        in_specs=[tokf, tokf, tokf, gf, tokb, tokb, tokb, gb, _resident((8, 256))] + [sq] * len(masks),
        out_specs=[pl.BlockSpec((R, 256), lambda b, c: (b * N + c, 0)),
                   pl.BlockSpec((R, 256), lambda b, c: (b * N + N - 1 - c, 0))],
        out_shape=[jax.ShapeDtypeStruct((B * S, 256), f32)] * 2,
        scratch_shapes=[pltpu.VMEM((2, LANES, LANES), f32), pltpu.VMEM((2, LANES, LANES), f32)],
        compiler_params=_cparams(("parallel", "arbitrary")),
        name="dn_chunk",
    )(q, k, v, gates, q, k, v, gates, prm, *masks)


def _dil_kernel(q_ref, k_ref, v_ref, kp0, kn0, kp1, kn1, kp2, kn2, vp0, vn0, vp1, vn1, vp2, vn2,
                o_ref, os_ref, ls_ref, *, ts, nb, n_prompt_tiles, prompt_seq_tiles, sample_seq_tiles):
    i = pl.program_id(0)
    in_prompt = i < n_prompt_tiles
    seq_tiles = jnp.where(in_prompt, prompt_seq_tiles, sample_seq_tiles)
    pos = jnp.where(in_prompt, i, i - n_prompt_tiles) % seq_tiles
    first, last = pos == 0, pos == seq_tiles - 1
    halos = ((kp0, kn0, vp0, vn0), (kp1, kn1, vp1, vn1), (kp2, kn2, vp2, vn2))
    lane = lax.broadcasted_iota(jnp.int32, (1, LANES), 1)
    low = lane < DIL_HD
    R = DIL_RADIUS
    items = []
    for gi, (_, d) in enumerate(DIL_GROUPS):
        n = ts // d
        qb = min(LANES, n)
        for r in range(d):
            for j in range(n // qb):
                for pr in range(2):
                    items.append((gi, d, n, qb, r, j, pr))

    def load_keys(main, prv, nxt, gi, d, n, qb, r, t0, pr):
        parts = []
        lo, hi = t0 - R, t0 + qb + R
        if lo < 0:
            parts.append(prv[pr, pl.ds(r + d * (lo + R), -lo, stride=d), :])
        a, b = max(lo, 0), min(hi, n)
        parts.append(main[2 * gi + pr, pl.ds(r + d * a, b - a, stride=d), :])
        if hi > n:
            parts.append(nxt[pr, pl.ds(r, hi - n, stride=d), :])
        return jnp.concatenate(parts, axis=0).astype(bf16)

    masks = {}

    def mask_for(qb, n, t0):
        key = (qb, n, t0)
        if key not in masks:
            nkeys = qb + 2 * R
            row = lax.broadcasted_iota(jnp.int32, (qb, nkeys), 0)
            col = lax.broadcasted_iota(jnp.int32, (qb, nkeys), 1)
            valid = jnp.abs(col - R - row) <= R
            if t0 - R < 0:
                valid = valid & ((col >= R - t0) | jnp.logical_not(first))
            if t0 + qb + R > n:
                valid = valid & ((col < n + R - t0) | jnp.logical_not(last))
            masks[key] = valid
        return masks[key]

    for b0 in range(0, len(items), nb):
        batch = items[b0:b0 + nb]
        qs, ks, vs, vl = [], [], [], []
        for (gi, d, n, qb, r, j, pr) in batch:
            t0 = j * qb
            kp, kn, vp, vn = halos[gi]
            qs.append(q_ref[2 * gi + pr, pl.ds(r + d * t0, qb, stride=d), :].astype(bf16))
            ks.append(load_keys(k_ref, kp, kn, gi, d, n, qb, r, t0, pr))
            vs.append(load_keys(v_ref, vp, vn, gi, d, n, qb, r, t0, pr))
            vl.append(mask_for(qb, n, t0))
        U = [(i_, hh) for i_ in range(len(batch)) for hh in range(2)]
        qm = [jnp.where(low if hh == 0 else ~low, qs[i_], jnp.zeros_like(qs[i_])) for (i_, hh) in U]
        sc = [jnp.where(vl[i_], _dot_nt(qm[u], ks[i_]), NEG_BIG) for u, (i_, hh) in enumerate(U)]
        mm = [jnp.max(x, axis=1, keepdims=True) for x in sc]
        ee = [jnp.exp(sc[u] - mm[u]) for u in range(len(U))]
        dd = [jnp.sum(x, axis=1, keepdims=True) for x in ee]
        pv = [_dot((ee[u] / dd[u]).astype(bf16), vs[i_]) for u, (i_, hh) in enumerate(U)]
        ll = [mm[u] + jnp.log(dd[u]) for u in range(len(U))]
        for i_, (gi, d, n, qb, r, j, pr) in enumerate(batch):
            dst = pl.ds(r + d * j * qb, qb, stride=d)
            os_ref[2 * gi + pr, dst, :] = jnp.where(low, pv[2 * i_], pv[2 * i_ + 1])
            ls_ref[2 * gi + pr, dst, :] = jnp.where(low, ll[2 * i_], ll[2 * i_ + 1])
    for pr in range(2):
        l0, l1, l2 = ls_ref[pr], ls_ref[2 + pr], ls_ref[4 + pr]
        mx = jnp.maximum(jnp.maximum(l0, l1), l2)
        e0, e1, e2 = jnp.exp(l0 - mx), jnp.exp(l1 - mx), jnp.exp(l2 - mx)
        y = (e0 * os_ref[pr] + e1 * os_ref[2 + pr] + e2 * os_ref[4 + pr]) / (e0 + e1 + e2)
        o_ref[:, pr * LANES:(pr + 1) * LANES] = y.astype(o_ref.dtype)


def _dilated(q, k, v, n_prompt_tiles, prompt_seq_tiles, sample_seq_tiles, ts, nb=8):
    T = q.shape[1]
    main = pl.BlockSpec((6, ts, LANES), lambda i: (0, i, 0))
    halo_specs = []
    for gi, (_, d) in enumerate(DIL_GROUPS):
        h = DIL_RADIUS * d
        per = ts // h
        halo_specs.append(pl.BlockSpec((2, h, LANES),
                                       lambda i, per=per, gi=gi: (gi, jnp.maximum(i * per - 1, 0), 0)))
        halo_specs.append(pl.BlockSpec((2, h, LANES),
                                       lambda i, per=per, gi=gi, h=h: (gi, jnp.minimum((i + 1) * per, T // h - 1), 0)))
    return pl.pallas_call(
        functools.partial(_dil_kernel, ts=ts, nb=nb, n_prompt_tiles=n_prompt_tiles,
                          prompt_seq_tiles=prompt_seq_tiles, sample_seq_tiles=sample_seq_tiles),
        grid=(T // ts,),
        in_specs=[main, main, main] + halo_specs + halo_specs,
        out_specs=pl.BlockSpec((ts, 256), lambda i: (i, 0)),
        out_shape=jax.ShapeDtypeStruct((T, 256), bf16),
        scratch_shapes=[pltpu.VMEM((6, ts, LANES), f32), pltpu.VMEM((6, ts, LANES), f32)],
        compiler_params=_cparams(("parallel",)),
        name="dilated",
    )(q, k, v, *([k] * 6), *([v] * 6))


def _merge_kernel(x_ref, g_ref, ya_ref, yb_ref, of_ref, ob_ref, z_ref, dng_ref, bd_ref, yd_ref,
                  wg_ref, wb_ref, wo_ref, out_ref):
    x = x_ref[...]
    hb = _rms(x, g_ref[...]).astype(bf16)
    o = of_ref[...] + ob_ref[...]
    ms = _group_sum64(o * o, bd_ref[...]) * (1.0 / DN_DV)
    yc = o * lax.rsqrt(ms + NORM_EPS) * dng_ref[...] * _silu(z_ref[...].astype(f32))
    ys = (ya_ref[...], yb_ref[...], yc.astype(bf16), yd_ref[...])
    merged = jnp.zeros_like(x)
    for n in range(4):
        merged = merged + jax.nn.sigmoid(_dot(hb, wg_ref[n])) * _dot(ys[n], wb_ref[n])
    out_ref[...] = x + _dot(merged.astype(bf16), wo_ref[...])


def _merge(x, g, ya, yb, of, ob, z, dng, bd, yd, wg, wb, wo, tm=512):
    T = x.shape[0]
    t256 = pl.BlockSpec((tm, 256), lambda i: (i, 0))
    return pl.pallas_call(
        _merge_kernel,
        grid=(T // tm,),
        in_specs=[pl.BlockSpec((tm, D_MODEL), lambda i: (i, 0)), _resident((1, D_MODEL)),
                  t256, t256, t256, t256, t256, _resident((1, 256)), _resident((256, 256)), t256,
                  _resident((4, D_MODEL, D_MODEL)), _resident((4, 256, D_MODEL)), _resident((D_MODEL, D_MODEL))],
        out_specs=pl.BlockSpec((tm, D_MODEL), lambda i: (i, 0)),
        out_shape=jax.ShapeDtypeStruct((T, D_MODEL), f32),
        compiler_params=_cparams(("parallel",)),
        name="merge",
    )(x, g, ya, yb, of, ob, z, dng, bd, yd, wg, wb, wo)


def _rope_tables(smax):
    pos = jnp.arange(smax, dtype=f32)[:, None]

    def cs(d):
        inv = ROPE_THETA ** (-jnp.arange(0, d, 2, dtype=f32) / d)
        ang = pos * inv[None, :]
        return jnp.cos(ang), jnp.sin(ang)

    def head_pattern(d):
        c, s = cs(d)
        z = jnp.zeros_like(s)
        return jnp.concatenate([c, c], 1), jnp.concatenate([-s, z], 1), jnp.concatenate([z, s], 1)

    c32, a32, b32 = head_pattern(32)
    c64, a64, b64 = head_pattern(64)
    one = jnp.ones((smax, 64), f32)
    z64 = jnp.zeros((smax, 64), f32)
    z32 = jnp.zeros((smax, 32), f32)
    set_a = [jnp.tile(t, (1, 4)) for t in (c32, a32, b32)]
    set_b = [jnp.concatenate([one, c32, z32], 1), jnp.concatenate([z64, a32, z32], 1),
             jnp.concatenate([z64, b32, z32], 1)]
    set_c = [jnp.tile(t, (1, 2)) for t in (c64, a64, b64)]
    return jnp.stack(set_a + set_b + set_c, 0)


def _head_pad(w, n_heads, width, lo, hi, at=0):
    k = w.shape[0]
    w = w.reshape(k, n_heads, width)[:, :, lo:hi]
    out = jnp.zeros((k, n_heads, LANES), w.dtype).at[:, :, at:at + hi - lo].set(w)
    return out.reshape(k, n_heads * LANES)


def _layer_weights(w, li):
    sizes = (MLA_Q_RANK, MLA_KV_RANK, MLA_ROPE, 256, 256, 256, DN_QKV, 4, 4, 4, 4, 256,
             DIL_COLS, DIL_COLS, DIL_COLS)
    offs = np.cumsum((0,) + sizes)
    win = w['w_in'][li]
    col = lambda n: win[:, offs[n]:offs[n + 1]]
    kr_pad = jnp.zeros((D_MODEL, LANES), f32).at[:, MLA_NOPE:MLA_NOPE + MLA_ROPE].set(col(2))
    gates = jnp.concatenate([col(7), col(8), col(9), col(10), jnp.zeros((D_MODEL, LANES - 16), f32)], 1)
    big = jnp.concatenate([col(0), col(1), kr_pad, col(3), col(4), _head_pad(col(5), 4, 64, 0, 64),
                           col(6), gates, col(11), col(12), col(13), col(14)], 1).astype(bf16)
    assert big.shape[1] == C_END
    ukv = w['mla_w_ukv'][li]
    out = dict(
        w_in=big,
        wuq=_head_pad(w['mla_w_uq'][li], 4, MLA_NOPE + MLA_ROPE, 0, MLA_NOPE + MLA_ROPE).astype(bf16),
        wuk=_head_pad(ukv, 4, MLA_NOPE + MLA_V, 0, MLA_NOPE).astype(bf16),
        wuv=_head_pad(ukv, 4, MLA_NOPE + MLA_V, MLA_NOPE, MLA_NOPE + MLA_V).astype(bf16),
    )
    for n in ('ff1_w1', 'ff1_w3', 'ff1_w2', 'ff2_w1', 'ff2_w3', 'ff2_w2', 'w_gate', 'w_branch', 'w_out',
              'ple_gate', 'ple_proj'):
        out[n] = w[n][li].astype(bf16)
    return out


def _dn_masks(n_heads):
    idx = np.arange(n_heads * DN_CHUNK)
    same = (idx[:, None] // DN_CHUNK) == (idx[None, :] // DN_CHUNK)
    i, j = idx[:, None] % DN_CHUNK, idx[None, :] % DN_CHUNK
    mk = lambda m: jnp.asarray((same & m).astype(np.float32))
    blk = lambda b: (i // b) == (j // b)
    return (mk(i >= j), mk(i > j), mk(i <= j), mk(i < j), mk(np.ones_like(same)),
            mk(blk(16)), mk(blk(32) & ~blk(16)), mk(~blk(32)))


def _seq_tile(s, pref):
    return min(s, pref)


def kernel(x_prompt, x_sample, p_prompt, p_sample, norm_ff1, ff1_w1, ff1_w3, ff1_w2, norm_mix, w_in, mla_q_norm, mla_kv_norm, mla_w_uq, mla_w_ukv, diff_lambda, diff_subln, dn_conv, dn_a_log, dn_dt_bias, dn_out_norm, w_branch, w_gate, w_out, norm_ff2, ff2_w1, ff2_w3, ff2_w2, norm_ple, ple_gate, ple_proj, norm_final):
    w = dict(norm_ff1=norm_ff1, ff1_w1=ff1_w1, ff1_w3=ff1_w3, ff1_w2=ff1_w2, norm_mix=norm_mix, w_in=w_in,
             mla_q_norm=mla_q_norm, mla_kv_norm=mla_kv_norm, mla_w_uq=mla_w_uq, mla_w_ukv=mla_w_ukv,
             diff_lambda=diff_lambda, diff_subln=diff_subln, dn_conv=dn_conv, dn_a_log=dn_a_log,
             dn_dt_bias=dn_dt_bias, dn_out_norm=dn_out_norm, w_branch=w_branch, w_gate=w_gate, w_out=w_out,
             norm_ff2=norm_ff2, ff2_w1=ff2_w1, ff2_w3=ff2_w3, ff2_w2=ff2_w2, norm_ple=norm_ple,
             ple_gate=ple_gate, ple_proj=ple_proj)
    depth = w_in.shape[0]
    BP, SP, _ = x_prompt.shape
    BS, SS, _ = x_sample.shape
    TP, TS = BP * SP, BS * SS
    T = TP + TS
    groups = ((BP, SP, 0), (BS, SS, TP))
    tm = min(512, SP, SS)
    assert TP % tm == 0 and TS % tm == 0 and SP % tm == 0 and SS % tm == 0

    xs = [x_prompt.reshape(TP, D_MODEL), x_sample.reshape(TS, D_MODEL)]
    ps = [p_prompt.reshape(depth, TP, PLE_DIM), p_sample.reshape(depth, TS, PLE_DIM)]
    npt = TP // tm

    tab = _rope_tables(max(SP, SS))
    masks = _dn_masks(2)
    bd_bf = _dn_masks(DN_HEADS)[4].astype(bf16)
    ones_pat = jnp.tile(jnp.concatenate([jnp.zeros((1, 64), f32), jnp.ones((1, 64), f32)], 1), (1, 4))
    row = lambda v: v.reshape(1, -1).astype(f32)
    mla_units = tuple((h, h, h) for h in range(MLA_HEADS))
    diff_units = tuple((hm, hm // 4, hm // 2) for hm in range(2 * DIFF_HEADS))
    no_sc = jnp.zeros((2,), f32)
    no_g = jnp.zeros((1, LANES), f32)

    for li in range(depth):
        lw = _layer_weights(w, li)
        x = _ffn(xs, row(norm_ff1[li]), lw['ff1_w1'], lw['ff1_w3'], lw['ff1_w2'], n_prompt_tiles=npt, tm=tm)

        (mq, mk, mv, dq, dk, dv, nqkv, ng, nz, lq, lk, lv) = _inproj(
            x, row(norm_mix[li]), lw['w_in'], row(mla_q_norm[li]), row(mla_kv_norm[li]),
            lw['wuq'], lw['wuk'], lw['wuv'], ones_pat, tab, TP // tm, SP // tm, SS // tm, tm)

        lf = diff_lambda[li].astype(f32)
        lambda_init = 0.8 - 0.6 * math.exp(-0.3 * li)
        lam = jnp.exp(jnp.sum(lf[0] * lf[1])) - jnp.exp(jnp.sum(lf[2] * lf[3])) + lambda_init
        sc = jnp.stack([lam, jnp.asarray(1.0 - lambda_init, f32)]).astype(f32)
        subln = jnp.tile(row(diff_subln[li]), (1, 2))
        ya, yb = [], []
        for (B, S, off) in groups:
            tq = _seq_tile(S, 2048)
            ya.append(_flash(mq, mk, mv, no_sc, no_g, B=B, S=S, off=off, units=mla_units, diff=False,
                             tq=tq, tk=_seq_tile(S, 2048), name="mla_attn"))
            yb.append(_flash(dq, dk, dv, sc, subln, B=B, S=S, off=off, units=diff_units, diff=True,
                             tq=tq, tk=_seq_tile(S, 2048), name="diff_attn"))
        ya, yb = jnp.concatenate(ya, 0), jnp.concatenate(yb, 0)

        conv_w = jnp.concatenate([dn_conv[li].astype(f32), jnp.zeros((8 - DN_CONV, DN_QKV), f32)], 0)
        nq, nk, nv = _dn_prep(nqkv, conv_w, bd_bf, TP // tm, SP // tm, SS // tm, tm)
        gates = ng[:, :16].reshape(T // DN_CHUNK, DN_CHUNK, 4, DN_HEADS)
        gates = jnp.transpose(gates, (0, 2, 3, 1)).reshape(T // DN_CHUNK, 4, 4 * DN_CHUNK)
        gates = jnp.concatenate([gates, jnp.zeros_like(gates)], 1)
        al, dtb = dn_a_log[li].astype(f32), dn_dt_bias[li].astype(f32)
        rep = lambda v: jnp.repeat(v, DN_CHUNK)[None, :]
        prm = jnp.concatenate([rep(jnp.exp(al[0])), rep(dtb[0]), rep(jnp.exp(al[1])), rep(dtb[1]),
                               jnp.zeros((4, 4 * DN_CHUNK), f32)], 0)
        of, ob = [], []
        for (B, S, off) in groups:
            f_, b_ = _dn_chunk(nq, nk, nv, gates, prm, masks, B=B, S=S, off=off)
            of.append(f_)
            ob.append(b_)
        of, ob = jnp.concatenate(of, 0), jnp.concatenate(ob, 0)

        ts = min(1024, SP, SS)
        yd = _dilated(lq, lk, lv, TP // ts, SP // ts, SS // ts, ts)

        x = _merge(x, row(norm_mix[li]), ya, yb, of, ob, nz, jnp.tile(row(dn_out_norm[li]), (1, 4)), bd_bf,
                   yd, lw['w_gate'], lw['w_branch'], lw['w_out'], tm=tm)
        last = li == depth - 1
        ple = (ps, li, row(norm_ple[li]), lw['ple_gate'], lw['ple_proj'], row(norm_final))
        x = _ffn([x], row(norm_ff2[li]), lw['ff2_w1'], lw['ff2_w3'], lw['ff2_w2'], n_prompt_tiles=npt, tm=tm,
                 ple=ple, final=last, split_out=last)
        xs = [x]

    return (x[0].reshape(BP, SP, D_MODEL), x[1].reshape(BS, SS, D_MODEL))
```

```python
import functools
import math

import numpy as np
import jax
import jax.numpy as jnp
from jax import lax
from jax.experimental import pallas as pl
from jax.experimental.pallas import tpu as pltpu

f32 = jnp.float32
bf16 = jnp.bfloat16

D_MODEL = 1024
PLE_DIM = 256
D_FF = 2816
ROPE_THETA = 10000.0
NORM_EPS = 1e-6
NEG_BIG = -1e30
LOG2E = math.log2(math.e)

MLA_HEADS = 4
MLA_Q_RANK = 256
MLA_KV_RANK = 128
MLA_NOPE = 64
MLA_ROPE = 32
MLA_V = 64

DIFF_HEADS = 4
DIFF_HD = 32
DIFF_VD = 64

DN_HEADS = 4
DN_DK = 64
DN_DV = 64
DN_CONV = 5
DN_CHUNK = 64
DN_QKV = DN_HEADS * (2 * DN_DK + DN_DV)

DIL_GROUPS = ((128, 1), (512, 4), (2048, 16))
DIL_HEADS = 4
DIL_HD = 64
DIL_RADIUS = 64
DIL_COLS = len(DIL_GROUPS) * DIL_HEADS * DIL_HD

LANES = 128
VMEM_LIMIT = 56 * 1024 * 1024

C_CQKV = 0
C_BQ = 512
C_BK = 768
C_BV = 1024
C_DNQKV = 1536
C_DNG = 2304
C_DNZ = 2432
C_DQ = 2688
C_DK = 3456
C_DV = 4224
C_END = 4992


def _cparams(sem):
    return pltpu.CompilerParams(dimension_semantics=sem, vmem_limit_bytes=VMEM_LIMIT)


def _resident(shape):
    nd = len(shape)
    return pl.BlockSpec(shape, lambda *_: (0,) * nd, pipeline_mode=pl.Buffered(1))


def _dot(a, b):
    return jnp.dot(a, b, preferred_element_type=f32)


def _dot_nt(a, b):
    return lax.dot_general(a, b, (((1,), (1,)), ((), ())), preferred_element_type=f32)


def _rms(x, g):
    ms = jnp.mean(x * x, axis=-1, keepdims=True)
    return x * lax.rsqrt(ms + NORM_EPS) * g


def _silu(x):
    return x * jax.nn.sigmoid(x)


def _split3_dot(x, m):
    x1 = x.astype(bf16)
    r1 = x - x1.astype(f32)
    x2 = r1.astype(bf16)
    x3 = (r1 - x2.astype(f32)).astype(bf16)
    return _dot(x1, m) + _dot(x2, m) + _dot(x3, m)


def _group_sum64(x, bd):
    return _split3_dot(x, bd)


def _rope128(x, c, s1, s2, half):
    return x * c + pltpu.roll(x, LANES - half, 1) * s1 + pltpu.roll(x, half, 1) * s2


def _ffn_kernel(*refs, fc, n_in, ple, final, n_out, n_prompt_tiles):
    refs = list(refs)
    take = lambda k: [refs.pop(0) for _ in range(k)]
    x_refs = take(n_in)
    g_ref, w1_ref, w3_ref, w2_ref = take(4)
    if ple:
        p_refs = take(2)
        gp_ref, wg_ref, wp_ref, gf_ref = take(4)
    o_refs = take(n_out)
    in_prompt = pl.program_id(0) < n_prompt_tiles

    def pick(rs):
        return rs[0][...] if len(rs) == 1 else jnp.where(in_prompt, rs[0][...], rs[1][...])

    x = pick(x_refs)
    xn = _rms(x, g_ref[...]).astype(bf16)
    y = jnp.zeros_like(x)
    for c in range(D_FF // fc):
        a = _dot(xn, w1_ref[:, c * fc:(c + 1) * fc])
        b = _dot(xn, w3_ref[:, c * fc:(c + 1) * fc])
        y = y + _dot((_silu(a) * b).astype(bf16), w2_ref[c * fc:(c + 1) * fc, :])
    y = x + 0.5 * y
    if ple:
        yn = _rms(y, gp_ref[...]).astype(bf16)
        gate = jax.nn.sigmoid(_dot(yn, wg_ref[...]))
        y = y + gate * _dot(pick(p_refs).astype(bf16), wp_ref[...])
        if final:
            y = _rms(y, gf_ref[...])
    if n_out == 1:
        o_refs[0][...] = y
    else:
        @pl.when(in_prompt)
        def _():
            o_refs[0][...] = y

        @pl.when(jnp.logical_not(in_prompt))
        def _():
            o_refs[1][...] = y


def _ffn(xs, g, w1, w3, w2, *, n_prompt_tiles, tm, ple=None, final=False, split_out=False, fc=256):
    n_in = len(xs)
    tp = n_prompt_tiles * tm
    T = sum(x.shape[0] for x in xs)

    def tok_specs(width, dual, lead=None):
        def mk(fn):
            if lead is None:
                return pl.BlockSpec((tm, width), lambda i: (fn(i), 0))
            return pl.BlockSpec((None, tm, width), lambda i: (lead, fn(i), 0))
        if not dual:
            return [mk(lambda i: i)]
        return [mk(lambda i: jnp.minimum(i, n_prompt_tiles - 1)), mk(lambda i: jnp.maximum(i - n_prompt_tiles, 0))]

    args = list(xs) + [g, w1, w3, w2]
    in_specs = tok_specs(D_MODEL, n_in == 2) + [_resident((1, D_MODEL)), _resident((D_MODEL, D_FF)),
                                                _resident((D_MODEL, D_FF)), _resident((D_FF, D_MODEL))]
    if ple is not None:
        ps, li, gp, wg, wp, gf = ple
        args += list(ps) + [gp, wg, wp, gf]
        in_specs += tok_specs(PLE_DIM, True, lead=li) + [_resident((1, D_MODEL)), _resident((D_MODEL, D_MODEL)),
                                                         _resident((PLE_DIM, D_MODEL)), _resident((1, D_MODEL))]
    if split_out:
        out_specs = [pl.BlockSpec((tm, D_MODEL), lambda i: (jnp.minimum(i, n_prompt_tiles - 1), 0)),
                     pl.BlockSpec((tm, D_MODEL), lambda i: (jnp.maximum(i - n_prompt_tiles, 0), 0))]
        out_shape = [jax.ShapeDtypeStruct((tp, D_MODEL), f32), jax.ShapeDtypeStruct((T - tp, D_MODEL), f32)]
    else:
        out_specs = pl.BlockSpec((tm, D_MODEL), lambda i: (i, 0))
        out_shape = jax.ShapeDtypeStruct((T, D_MODEL), f32)
    return pl.pallas_call(
        functools.partial(_ffn_kernel, fc=fc, n_in=n_in, ple=ple is not None, final=final,
                          n_out=2 if split_out else 1, n_prompt_tiles=n_prompt_tiles),
        grid=(T // tm,),
        in_specs=in_specs,
        out_specs=out_specs,
        out_shape=out_shape,
        compiler_params=_cparams(("arbitrary",) if split_out else ("parallel",)),
        name="ffn_ple" if ple is not None else "ffn",
    )(*args)


def _inproj_kernel(x_ref, g_ref, w_ref, qn_ref, kvn_ref, wuq_ref, wuk_ref, wuv_ref, ones_ref, tab_ref,
                   mq_ref, mk_ref, mv_ref, dq_ref, dk_ref, dv_ref, nqkv_ref, ng_ref, nz_ref,
                   lq_ref, lk_ref, lv_ref):
    hb = _rms(x_ref[...], g_ref[...]).astype(bf16)

    def proj(lo, hi):
        return _dot(hb, w_ref[:, lo:hi])

    def tabs(s):
        return tab_ref[3 * s], tab_ref[3 * s + 1], tab_ref[3 * s + 2]

    ones_pat = ones_ref[...]

    c = proj(C_CQKV, C_CQKV + 512)
    cq, ckv, krp = c[:, :MLA_Q_RANK], c[:, MLA_Q_RANK:MLA_Q_RANK + MLA_KV_RANK], c[:, 384:512]
    q = _dot(_rms(cq, qn_ref[...]).astype(bf16), wuq_ref[...])
    kvn = _rms(ckv, kvn_ref[...]).astype(bf16)
    kn = _dot(kvn, wuk_ref[...])
    cb, s1b, s2b = tabs(1)
    q_scale = (MLA_NOPE + MLA_ROPE) ** -0.5 * LOG2E
    kr = _rope128(krp, cb, s1b, s2b, MLA_ROPE // 2)
    for h in range(MLA_HEADS):
        sl = slice(h * LANES, (h + 1) * LANES)
        mq_ref[:, sl] = (_rope128(q[:, sl], cb, s1b, s2b, MLA_ROPE // 2) * q_scale).astype(bf16)
        mk_ref[:, sl] = (kn[:, sl] + kr).astype(bf16)
    mv_ref[...] = (_dot(kvn, wuv_ref[...]) + ones_pat).astype(bf16)

    ca, s1a, s2a = tabs(0)
    lane = lax.broadcasted_iota(jnp.int32, (1, LANES), 1)
    bq = proj(C_BQ, C_BQ + 256)
    bk = proj(C_BK, C_BK + 256)
    for ch in range(2):
        sl = slice(ch * LANES, (ch + 1) * LANES)
        qr = _rope128(bq[:, sl], ca, s1a, s2a, DIFF_HD // 2) * (DIFF_HD ** -0.5 * LOG2E)
        for slot in range(4):
            hm = ch * 4 + slot
            dq_ref[:, hm * LANES:(hm + 1) * LANES] = jnp.where(lane // DIFF_HD == slot, qr, 0.0).astype(bf16)
        dk_ref[:, sl] = _rope128(bk[:, sl], ca, s1a, s2a, DIFF_HD // 2).astype(bf16)
    dv_ref[...] = (proj(C_BV, C_BV + 512) + ones_pat).astype(bf16)

    nqkv_ref[...] = proj(C_DNQKV, C_DNQKV + DN_QKV)
    ng_ref[...] = proj(C_DNG, C_DNG + LANES)
    nz_ref[...] = proj(C_DNZ, C_DNZ + 256).astype(bf16)

    cc, s1c, s2c = tabs(2)
    lq = proj(C_DQ, C_DQ + DIL_COLS)
    lk = proj(C_DK, C_DK + DIL_COLS)
    lv = proj(C_DV, C_DV + DIL_COLS)
    for ch in range(DIL_COLS // LANES):
        sl = slice(ch * LANES, (ch + 1) * LANES)
        lq_ref[ch] = _rope128(lq[:, sl], cc, s1c, s2c, DIL_HD // 2) * (DIL_HD ** -0.5)
        lk_ref[ch] = _rope128(lk[:, sl], cc, s1c, s2c, DIL_HD // 2)
        lv_ref[ch] = lv[:, sl]


def _inproj(x, g, w, qn, kvn, wuq, wuk, wuv, ones_pat, tab, n_prompt_tiles, prompt_pos_tiles, sample_pos_tiles,
            tm):
    T = x.shape[0]

    def pos_map(i):
        return (0, jnp.where(i < n_prompt_tiles, i % prompt_pos_tiles, (i - n_prompt_tiles) % sample_pos_tiles), 0)

    def tok(w_, dt):
        return pl.BlockSpec((tm, w_), lambda i: (i, 0)), jax.ShapeDtypeStruct((T, w_), dt)

    dil = (pl.BlockSpec((DIL_COLS // LANES, tm, LANES), lambda i: (0, i, 0)),
           jax.ShapeDtypeStruct((DIL_COLS // LANES, T, LANES), f32))
    outs = [tok(512, bf16), tok(512, bf16), tok(512, bf16),
            tok(1024, bf16), tok(256, bf16), tok(512, bf16),
            tok(DN_QKV, f32), tok(LANES, f32), tok(256, bf16),
            dil, dil, dil]
    return pl.pallas_call(
        _inproj_kernel,
        grid=(T // tm,),
        in_specs=[pl.BlockSpec((tm, D_MODEL), lambda i: (i, 0)),
                  _resident((1, D_MODEL)), _resident((D_MODEL, C_END)),
                  _resident((1, MLA_Q_RANK)), _resident((1, MLA_KV_RANK)),
                  _resident((MLA_Q_RANK, 512)), _resident((MLA_KV_RANK, 512)), _resident((MLA_KV_RANK, 512)),
                  _resident((1, 512)),
                  pl.BlockSpec((9, tm, LANES), pos_map)],
        out_specs=[o[0] for o in outs],
        out_shape=[o[1] for o in outs],
        compiler_params=_cparams(("parallel",)),
        name="inproj",
    )(x, g, w, qn, kvn, wuq, wuk, wuv, ones_pat, tab)


def _flash_kernel(sc_ref, g_ref, q_ref, k_ref, v_ref, o_ref, m_ref, acc_ref, *, units, diff, nk):
    ki = pl.program_id(2)

    @pl.when(ki == 0)
    def _():
        m_ref[...] = jnp.full(m_ref.shape, NEG_BIG, f32)
        acc_ref[...] = jnp.zeros(acc_ref.shape, f32)

    nc = k_ref.shape[0] // LANES
    for u, (qi, kc, vh) in enumerate(units):
        q = q_ref[:, qi * LANES:(qi + 1) * LANES]
        k = k_ref[:, kc * LANES:(kc + 1) * LANES]
        s = _dot_nt(q, k)
        m_prev = m_ref[u]
        mx = s[:, 0:LANES]
        for c in range(1, nc):
            mx = jnp.maximum(mx, s[:, c * LANES:(c + 1) * LANES])
        m_new = jnp.maximum(m_prev, jnp.max(mx, axis=1, keepdims=True))
        alpha = jnp.exp2(m_prev - m_new)
        p = jnp.concatenate([jnp.exp2(s[:, c * LANES:(c + 1) * LANES] - m_new).astype(bf16)
                             for c in range(nc)], axis=1)
        acc_ref[u] = alpha * acc_ref[u] + _dot(p, v_ref[:, vh * LANES:(vh + 1) * LANES])
        m_ref[u] = m_new

    @pl.when(ki == nk - 1)
    def _():
        lane = lax.broadcasted_iota(jnp.int32, (1, LANES), 1)
        low = lane < 64

        def normed(u):
            a = acc_ref[u]
            return a / pltpu.roll(a, 64, 1)

        heads = []
        if diff:
            lam, post = sc_ref[0], sc_ref[1]
            for h in range(DIFF_HEADS):
                o = normed(2 * h) - lam * normed(2 * h + 1)
                ms = jnp.sum(jnp.where(low, o * o, 0.0), axis=1, keepdims=True) * (1.0 / DIFF_VD)
                heads.append(o * lax.rsqrt(ms + NORM_EPS) * g_ref[...] * post)
        else:
            heads = [normed(u) for u in range(len(units))]
        for pr in range(2):
            pair = jnp.where(low, heads[2 * pr], pltpu.roll(heads[2 * pr + 1], 64, 1))
            o_ref[:, pr * LANES:(pr + 1) * LANES] = pair.astype(o_ref.dtype)


def _flash(q, k, v, sc, g, *, B, S, off, units, diff, tq, tk, name):
    nq, nk = S // tq, S // tk
    oq, ok = off // tq, off // tk
    n_acc = len(units)
    return pl.pallas_call(
        functools.partial(_flash_kernel, units=units, diff=diff, nk=nk),
        grid=(B, nq, nk),
        in_specs=[pl.BlockSpec(memory_space=pltpu.SMEM),
                  pl.BlockSpec((1, LANES), lambda b, i, j: (0, 0)),
                  pl.BlockSpec((tq, q.shape[1]), lambda b, i, j: (oq + b * nq + i, 0)),
                  pl.BlockSpec((tk, k.shape[1]), lambda b, i, j: (ok + b * nk + j, 0)),
                  pl.BlockSpec((tk, v.shape[1]), lambda b, i, j: (ok + b * nk + j, 0))],
        out_specs=pl.BlockSpec((tq, 256), lambda b, i, j: (b * nq + i, 0)),
        out_shape=jax.ShapeDtypeStruct((B * S, 256), bf16),
        scratch_shapes=[pltpu.VMEM((n_acc, tq, LANES), f32), pltpu.VMEM((n_acc, tq, LANES), f32)],
        compiler_params=_cparams(("parallel", "parallel", "arbitrary")),
        name=name,
    )(sc, g, q, k, v)


def _dn_prep_kernel(x_ref, xp_ref, xn_ref, w_ref, bd_ref, q_ref, k_ref, v_ref, buf_ref, *,
                    tm, n_prompt_tiles, prompt_seq_tiles, sample_seq_tiles):
    i = pl.program_id(0)
    seq_tiles = jnp.where(i < n_prompt_tiles, prompt_seq_tiles, sample_seq_tiles)
    pos = jnp.where(i < n_prompt_tiles, i, i - n_prompt_tiles) % seq_tiles
    buf_ref[0:8, :] = jnp.where(pos == 0, 0.0, xp_ref[...])
    buf_ref[8:8 + tm, :] = x_ref[...]
    buf_ref[8 + tm:16 + tm, :] = jnp.where(pos == seq_tiles - 1, 0.0, xn_ref[...])
    pad = (DN_CONV - 1) // 2
    acc = jnp.zeros((tm, DN_QKV), f32)
    for t in range(DN_CONV):
        acc = acc + buf_ref[pl.ds(8 - pad + t, tm), :] * w_ref[t:t + 1, :]
    act = _silu(acc)
    bd = bd_ref[...]
    hk = DN_HEADS * DN_DK
    q, k = act[:, :hk], act[:, hk:2 * hk]
    q_ref[...] = q * lax.rsqrt(_group_sum64(q * q, bd) + NORM_EPS) * (DN_DK ** -0.5)
    k_ref[...] = k * lax.rsqrt(_group_sum64(k * k, bd) + NORM_EPS)
    v_ref[...] = act[:, 2 * hk:]


def _dn_prep(x, w, bd, n_prompt_tiles, prompt_seq_tiles, sample_seq_tiles, tm):
    T = x.shape[0]
    nb8 = tm // 8
    o = (pl.BlockSpec((tm, 256), lambda i: (i, 0)), jax.ShapeDtypeStruct((T, 256), f32))
    return pl.pallas_call(
        functools.partial(_dn_prep_kernel, tm=tm, n_prompt_tiles=n_prompt_tiles,
                          prompt_seq_tiles=prompt_seq_tiles, sample_seq_tiles=sample_seq_tiles),
        grid=(T // tm,),
        in_specs=[pl.BlockSpec((tm, DN_QKV), lambda i: (i, 0)),
                  pl.BlockSpec((8, DN_QKV), lambda i: (jnp.maximum(i * nb8 - 1, 0), 0)),
                  pl.BlockSpec((8, DN_QKV), lambda i: (jnp.minimum((i + 1) * nb8, T // 8 - 1), 0)),
                  _resident((8, DN_QKV)), _resident((256, 256))],
        out_specs=[o[0]] * 3,
        out_shape=[o[1]] * 3,
        scratch_shapes=[pltpu.VMEM((tm + 16, DN_QKV), f32)],
        compiler_params=_cparams(("parallel",)),
        name="dn_prep",
    )(x, x, x, w, bd)


def _dn_prep_chunks(chains, bd, bdb, blocks):
    n = 2 * DN_CHUNK
    R = range(len(chains))
    rid = lax.broadcasted_iota(jnp.int32, (n, n), 0)
    lane = lax.broadcasted_iota(jnp.int32, (1, LANES), 1)
    low = lane < 64
    on = bd > 0.0

    def fold(x):
        return x + pltpu.roll(x, 64, 1)

    g_rows, beta_rows = [], []
    for (q, k, v, a_row, b_row, eal_row, dtb_row, mi, ms, um) in chains:
        xa = a_row + dtb_row
        g_rows.append(-eal_row * (jnp.maximum(xa, 0.0) + jnp.log1p(jnp.exp(-jnp.abs(xa)))))
        beta_rows.append(jax.nn.sigmoid(b_row))
    g8 = [jnp.broadcast_to(g, (8, n)) for g in g_rows]
    gc_rows = [_split3_dot(g8[c], chains[c][9])[0:1] for c in R]
    gsum_rows = [_split3_dot(g8[c], bdb)[0:1] for c in R]
    cols = [jnp.where(rid == 0, gc_rows[c], jnp.where(rid == 1, beta_rows[c], 0.0)).T for c in R]
    cg = [jnp.broadcast_to(cols[c][:, 0:1], (n, n)) for c in R]
    cb = [jnp.broadcast_to(cols[c][:, 1:2], (n, n)) for c in R]
    decay = [jnp.exp(jnp.where(chains[c][7] > 0.0, cg[c] - gc_rows[c], NEG_BIG)) for c in R]
    eg = [jnp.exp(cg[c]) for c in R]
    k4 = [jnp.tile(chains[c][1], (2, 1)) * bd for c in R]
    kb = [k4[c] * cb[c] for c in R]
    k4b = [k4[c].astype(bf16) for c in R]
    a_mat = [_dot_nt(kb[c].astype(bf16), k4b[c]) * decay[c] * chains[c][8] for c in R]
    rhs = [jnp.where(low, fold(jnp.tile(chains[c][2], (2, 1)) * bd * cb[c]), fold(kb[c] * eg[c])) for c in R]
    s16, m32, m64 = blocks
    eye = (rid == lax.broadcasted_iota(jnp.int32, (n, n), 1)).astype(f32)
    pw = [-(a_mat[c] * s16) for c in R]
    dm = [eye + pw[c] for c in R]
    for t in range(3):
        pb = [p.astype(bf16) for p in pw]
        pw = [_dot(pb[c], pb[c]) for c in R]
        dm = [dm[c] + _dot(pw[c].astype(bf16), dm[c].astype(bf16)) for c in R]
    for msk in (m32, m64):
        db = [d.astype(bf16) for d in dm]
        off = [_dot((a_mat[c] * msk).astype(bf16), db[c]) for c in R]
        dm = [dm[c] - _dot(db[c], off[c].astype(bf16)) for c in R]
    x = [_dot(dm[c].astype(bf16), rhs[c].astype(bf16)) for c in R]
    q4 = [jnp.tile(chains[c][0], (2, 1)) * bd for c in R]
    intra = [(_dot_nt(q4[c].astype(bf16), k4b[c]) * decay[c]).astype(bf16) for c in R]
    out = []
    for c in R:
        xr = pltpu.roll(x[c], 64, 1)
        u_bd = jnp.where(low, x[c], xr) * bd
        w_bd = (jnp.where(low, xr, x[c]) * bd).astype(bf16)
        ekd = jnp.exp(jnp.where(on, gsum_rows[c] - cg[c], 0.0))
        out.append((u_bd, w_bd, (q4[c] * eg[c]).astype(bf16), intra[c], (k4[c] * ekd).T.astype(bf16),
                    jnp.exp(gsum_rows[c])))
    return out


def _dn_chunk_kernel(qf_ref, kf_ref, vf_ref, gf_ref, qb_ref, kb_ref, vb_ref, gb_ref, prm_ref,
                     mif_ref, msf_ref, mib_ref, msb_ref, bd_ref, s16_ref, m32_ref, m64_ref,
                     of_ref, ob_ref, sf_ref, sb_ref, *, nch):
    @pl.when(pl.program_id(1) == 0)
    def _():
        sf_ref[...] = jnp.zeros(sf_ref.shape, f32)
        sb_ref[...] = jnp.zeros(sb_ref.shape, f32)

    C = DN_CHUNK
    bd = bd_ref[...]
    bdb = bd.astype(bf16)
    mif, mib = mif_ref[...], mib_ref[...]
    umf, umb = mib.astype(bf16), mif.astype(bf16)
    msf, msb = msf_ref[...], msb_ref[...]
    chains = []
    for (q_ref, k_ref, v_ref, g_ref, ga, gb_, pa, mi, ms, um) in (
            (qf_ref, kf_ref, vf_ref, gf_ref, 0, 1, 0, mif, msf, umf),
            (qb_ref, kb_ref, vb_ref, gb_ref, 2, 3, 2, mib, msb, umb)):
        for c in range(nch):
            r = slice(c * C, (c + 1) * C)
            for p in range(2):
                l = slice(p * LANES, (p + 1) * LANES)
                chains.append((q_ref[r, l], k_ref[r, l], v_ref[r, l], g_ref[c, ga:ga + 1, l], g_ref[c, gb_:gb_ + 1, l],
                               prm_ref[pa:pa + 1, l], prm_ref[pa + 1:pa + 2, l], mi, ms, um))
    pre = _dn_prep_chunks(chains, bd, bdb, (s16_ref[...], m32_ref[...], m64_ref[...]))
    half = 2 * nch
    sts = [sf_ref[0], sf_ref[1], sb_ref[0], sb_ref[1]]
    J = range(4)
    for c in range(nch):
        cr = nch - 1 - c
        idx = [(c, of_ref, 0, 0), (c, of_ref, 0, 1), (cr, ob_ref, half, 0), (cr, ob_ref, half, 1)]
        prs = [pre[base + 2 * cc + p] for (cc, _, base, p) in idx]
        sbs = [sts[j].astype(bf16) for j in J]
        vn = [prs[j][0] - _dot(prs[j][1], sbs[j]) for j in J]
        vb = [vn[j].astype(bf16) for j in J]
        obd = [_dot(prs[j][2], sbs[j]) + _dot(prs[j][3], vb[j]) for j in J]
        sts = [sts[j] * prs[j][5] + _dot(prs[j][4], vb[j]) for j in J]
        for j, (cc, o_ref, base, p) in enumerate(idx):
            o_ref[cc * C:(cc + 1) * C, p * LANES:(p + 1) * LANES] = obd[j][0:64] + obd[j][64:128]
    sf_ref[0], sf_ref[1], sb_ref[0], sb_ref[1] = sts


def _dn_chunk(q, k, v, gates, prm, masks, *, B, S, off, nch=4):
    R = DN_CHUNK * nch
    N = S // R
    oc = off // R
    fwd = lambda b, c: (oc + b * N + c, 0)
    bwd = lambda b, c: (oc + b * N + N - 1 - c, 0)
    tokf = pl.BlockSpec((R, 256), fwd)
    tokb = pl.BlockSpec((R, 256), bwd)
    gf = pl.BlockSpec((nch, 8, 256), lambda b, c: (oc + b * N + c, 0, 0))
    gb = pl.BlockSpec((nch, 8, 256), lambda b, c: (oc + b * N + N - 1 - c, 0, 0))
    sq = _resident((LANES, LANES))
    return pl.pallas_call(
        functools.partial(_dn_chunk_kernel, nch=nch),
        grid=(B, N),
        in_specs=[tokf, tokf, tokf, gf, tokb, tokb, tokb, gb, _resident((8, 256))] + [sq] * len(masks),
        out_specs=[pl.BlockSpec((R, 256), lambda b, c: (b * N + c, 0)),
                   pl.BlockSpec((R, 256), lambda b, c: (b * N + N - 1 - c, 0))],
        out_shape=[jax.ShapeDtypeStruct((B * S, 256), f32)] * 2,
        scratch_shapes=[pltpu.VMEM((2, LANES, LANES), f32), pltpu.VMEM((2, LANES, LANES), f32)],
        compiler_params=_cparams(("parallel", "arbitrary")),
        name="dn_chunk",
    )(q, k, v, gates, q, k, v, gates, prm, *masks)


def _dil_kernel(q_ref, k_ref, v_ref, kp0, kn0, kp1, kn1, kp2, kn2, vp0, vn0, vp1, vn1, vp2, vn2,
                o_ref, os_ref, ls_ref, *, ts, nb, n_prompt_tiles, prompt_seq_tiles, sample_seq_tiles):
    i = pl.program_id(0)
    in_prompt = i < n_prompt_tiles
    seq_tiles = jnp.where(in_prompt, prompt_seq_tiles, sample_seq_tiles)
    pos = jnp.where(in_prompt, i, i - n_prompt_tiles) % seq_tiles
    first, last = pos == 0, pos == seq_tiles - 1
    halos = ((kp0, kn0, vp0, vn0), (kp1, kn1, vp1, vn1), (kp2, kn2, vp2, vn2))
    lane = lax.broadcasted_iota(jnp.int32, (1, LANES), 1)
    low = lane < DIL_HD
    R = DIL_RADIUS
    items = []
    for gi, (_, d) in enumerate(DIL_GROUPS):
        n = ts // d
        qb = min(LANES, n)
        for r in range(d):
            for j in range(n // qb):
                for pr in range(2):
                    items.append((gi, d, n, qb, r, j, pr))

    def load_keys(main, prv, nxt, gi, d, n, qb, r, t0, pr):
        parts = []
        lo, hi = t0 - R, t0 + qb + R
        if lo < 0:
            parts.append(prv[pr, pl.ds(r + d * (lo + R), -lo, stride=d), :])
        a, b = max(lo, 0), min(hi, n)
        parts.append(main[2 * gi + pr, pl.ds(r + d * a, b - a, stride=d), :])
        if hi > n:
            parts.append(nxt[pr, pl.ds(r, hi - n, stride=d), :])
        return jnp.concatenate(parts, axis=0).astype(bf16)

    masks = {}

    def mask_for(qb, n, t0):
        key = (qb, n, t0)
        if key not in masks:
            nkeys = qb + 2 * R
            row = lax.broadcasted_iota(jnp.int32, (qb, nkeys), 0)
            col = lax.broadcasted_iota(jnp.int32, (qb, nkeys), 1)
            valid = jnp.abs(col - R - row) <= R
            if t0 - R < 0:
                valid = valid & ((col >= R - t0) | jnp.logical_not(first))
            if t0 + qb + R > n:
                valid = valid & ((col < n + R - t0) | jnp.logical_not(last))
            masks[key] = valid
        return masks[key]

    for b0 in range(0, len(items), nb):
        batch = items[b0:b0 + nb]
        qs, ks, vs, vl = [], [], [], []
        for (gi, d, n, qb, r, j, pr) in batch:
            t0 = j * qb
            kp, kn, vp, vn = halos[gi]
            qs.append(q_ref[2 * gi + pr, pl.ds(r + d * t0, qb, stride=d), :].astype(bf16))
            ks.append(load_keys(k_ref, kp, kn, gi, d, n, qb, r, t0, pr))
            vs.append(load_keys(v_ref, vp, vn, gi, d, n, qb, r, t0, pr))
            vl.append(mask_for(qb, n, t0))
        U = [(i_, hh) for i_ in range(len(batch)) for hh in range(2)]
        qm = [jnp.where(low if hh == 0 else ~low, qs[i_], jnp.zeros_like(qs[i_])) for (i_, hh) in U]
        sc = [jnp.where(vl[i_], _dot_nt(qm[u], ks[i_]), NEG_BIG) for u, (i_, hh) in enumerate(U)]
        mm = [jnp.max(x, axis=1, keepdims=True) for x in sc]
        ee = [jnp.exp(sc[u] - mm[u]) for u in range(len(U))]
        dd = [jnp.sum(x, axis=1, keepdims=True) for x in ee]
        pv = [_dot((ee[u] / dd[u]).astype(bf16), vs[i_]) for u, (i_, hh) in enumerate(U)]
        ll = [mm[u] + jnp.log(dd[u]) for u in range(len(U))]
        for i_, (gi, d, n, qb, r, j, pr) in enumerate(batch):
            dst = pl.ds(r + d * j * qb, qb, stride=d)
            os_ref[2 * gi + pr, dst, :] = jnp.where(low, pv[2 * i_], pv[2 * i_ + 1])
            ls_ref[2 * gi + pr, dst, :] = jnp.where(low, ll[2 * i_], ll[2 * i_ + 1])
    for pr in range(2):
        l0, l1, l2 = ls_ref[pr], ls_ref[2 + pr], ls_ref[4 + pr]
        mx = jnp.maximum(jnp.maximum(l0, l1), l2)
        e0, e1, e2 = jnp.exp(l0 - mx), jnp.exp(l1 - mx), jnp.exp(l2 - mx)
        y = (e0 * os_ref[pr] + e1 * os_ref[2 + pr] + e2 * os_ref[4 + pr]) / (e0 + e1 + e2)
        o_ref[:, pr * LANES:(pr + 1) * LANES] = y.astype(o_ref.dtype)


def _dilated(q, k, v, n_prompt_tiles, prompt_seq_tiles, sample_seq_tiles, ts, nb=8):
    T = q.shape[1]
    main = pl.BlockSpec((6, ts, LANES), lambda i: (0, i, 0))
    halo_specs = []
    for gi, (_, d) in enumerate(DIL_GROUPS):
        h = DIL_RADIUS * d
        per = ts // h
        halo_specs.append(pl.BlockSpec((2, h, LANES),
                                       lambda i, per=per, gi=gi: (gi, jnp.maximum(i * per - 1, 0), 0)))
        halo_specs.append(pl.BlockSpec((2, h, LANES),
                                       lambda i, per=per, gi=gi, h=h: (gi, jnp.minimum((i + 1) * per, T // h - 1), 0)))
    return pl.pallas_call(
        functools.partial(_dil_kernel, ts=ts, nb=nb, n_prompt_tiles=n_prompt_tiles,
                          prompt_seq_tiles=prompt_seq_tiles, sample_seq_tiles=sample_seq_tiles),
        grid=(T // ts,),
        in_specs=[main, main, main] + halo_specs + halo_specs,
        out_specs=pl.BlockSpec((ts, 256), lambda i: (i, 0)),
        out_shape=jax.ShapeDtypeStruct((T, 256), bf16),
        scratch_shapes=[pltpu.VMEM((6, ts, LANES), f32), pltpu.VMEM((6, ts, LANES), f32)],
        compiler_params=_cparams(("parallel",)),
        name="dilated",
    )(q, k, v, *([k] * 6), *([v] * 6))


def _merge_kernel(x_ref, g_ref, ya_ref, yb_ref, of_ref, ob_ref, z_ref, dng_ref, bd_ref, yd_ref,
                  wg_ref, wb_ref, wo_ref, out_ref):
    x = x_ref[...]
    hb = _rms(x, g_ref[...]).astype(bf16)
    o = of_ref[...] + ob_ref[...]
    ms = _group_sum64(o * o, bd_ref[...]) * (1.0 / DN_DV)
    yc = o * lax.rsqrt(ms + NORM_EPS) * dng_ref[...] * _silu(z_ref[...].astype(f32))
    ys = (ya_ref[...], yb_ref[...], yc.astype(bf16), yd_ref[...])
    merged = jnp.zeros_like(x)
    for n in range(4):
        merged = merged + jax.nn.sigmoid(_dot(hb, wg_ref[n])) * _dot(ys[n], wb_ref[n])
    out_ref[...] = x + _dot(merged.astype(bf16), wo_ref[...])


def _merge(x, g, ya, yb, of, ob, z, dng, bd, yd, wg, wb, wo, tm=512):
    T = x.shape[0]
    t256 = pl.BlockSpec((tm, 256), lambda i: (i, 0))
    return pl.pallas_call(
        _merge_kernel,
        grid=(T // tm,),
        in_specs=[pl.BlockSpec((tm, D_MODEL), lambda i: (i, 0)), _resident((1, D_MODEL)),
                  t256, t256, t256, t256, t256, _resident((1, 256)), _resident((256, 256)), t256,
                  _resident((4, D_MODEL, D_MODEL)), _resident((4, 256, D_MODEL)), _resident((D_MODEL, D_MODEL))],
        out_specs=pl.BlockSpec((tm, D_MODEL), lambda i: (i, 0)),
        out_shape=jax.ShapeDtypeStruct((T, D_MODEL), f32),
        compiler_params=_cparams(("parallel",)),
        name="merge",
    )(x, g, ya, yb, of, ob, z, dng, bd, yd, wg, wb, wo)


def _rope_tables(smax):
    pos = jnp.arange(smax, dtype=f32)[:, None]

    def cs(d):
        inv = ROPE_THETA ** (-jnp.arange(0, d, 2, dtype=f32) / d)
        ang = pos * inv[None, :]
        return jnp.cos(ang), jnp.sin(ang)

    def head_pattern(d):
        c, s = cs(d)
        z = jnp.zeros_like(s)
        return jnp.concatenate([c, c], 1), jnp.concatenate([-s, z], 1), jnp.concatenate([z, s], 1)

    c32, a32, b32 = head_pattern(32)
    c64, a64, b64 = head_pattern(64)
    one = jnp.ones((smax, 64), f32)
    z64 = jnp.zeros((smax, 64), f32)
    z32 = jnp.zeros((smax, 32), f32)
    set_a = [jnp.tile(t, (1, 4)) for t in (c32, a32, b32)]
    set_b = [jnp.concatenate([one, c32, z32], 1), jnp.concatenate([z64, a32, z32], 1),
             jnp.concatenate([z64, b32, z32], 1)]
    set_c = [jnp.tile(t, (1, 2)) for t in (c64, a64, b64)]
    return jnp.stack(set_a + set_b + set_c, 0)


def _head_pad(w, n_heads, width, lo, hi, at=0):
    k = w.shape[0]
    w = w.reshape(k, n_heads, width)[:, :, lo:hi]
    out = jnp.zeros((k, n_heads, LANES), w.dtype).at[:, :, at:at + hi - lo].set(w)
    return out.reshape(k, n_heads * LANES)


def _layer_weights(w, li):
    sizes = (MLA_Q_RANK, MLA_KV_RANK, MLA_ROPE, 256, 256, 256, DN_QKV, 4, 4, 4, 4, 256,
             DIL_COLS, DIL_COLS, DIL_COLS)
    offs = np.cumsum((0,) + sizes)
    win = w['w_in'][li]
    col = lambda n: win[:, offs[n]:offs[n + 1]]
    kr_pad = jnp.zeros((D_MODEL, LANES), f32).at[:, MLA_NOPE:MLA_NOPE + MLA_ROPE].set(col(2))
    gates = jnp.concatenate([col(7), col(8), col(9), col(10), jnp.zeros((D_MODEL, LANES - 16), f32)], 1)
    big = jnp.concatenate([col(0), col(1), kr_pad, col(3), col(4), _head_pad(col(5), 4, 64, 0, 64),
                           col(6), gates, col(11), col(12), col(13), col(14)], 1).astype(bf16)
    assert big.shape[1] == C_END
    ukv = w['mla_w_ukv'][li]
    out = dict(
        w_in=big,
        wuq=_head_pad(w['mla_w_uq'][li], 4, MLA_NOPE + MLA_ROPE, 0, MLA_NOPE + MLA_ROPE).astype(bf16),
        wuk=_head_pad(ukv, 4, MLA_NOPE + MLA_V, 0, MLA_NOPE).astype(bf16),
        wuv=_head_pad(ukv, 4, MLA_NOPE + MLA_V, MLA_NOPE, MLA_NOPE + MLA_V).astype(bf16),
    )
    for n in ('ff1_w1', 'ff1_w3', 'ff1_w2', 'ff2_w1', 'ff2_w3', 'ff2_w2', 'w_gate', 'w_branch', 'w_out',
              'ple_gate', 'ple_proj'):
        out[n] = w[n][li].astype(bf16)
    return out


def _dn_masks(n_heads):
    idx = np.arange(n_heads * DN_CHUNK)
    same = (idx[:, None] // DN_CHUNK) == (idx[None, :] // DN_CHUNK)
    i, j = idx[:, None] % DN_CHUNK, idx[None, :] % DN_CHUNK
    mk = lambda m: jnp.asarray((same & m).astype(np.float32))
    blk = lambda b: (i // b) == (j // b)
    return (mk(i >= j), mk(i > j), mk(i <= j), mk(i < j), mk(np.ones_like(same)),
            mk(blk(16)), mk(blk(32) & ~blk(16)), mk(~blk(32)))


def _seq_tile(s, pref):
    return min(s, pref)


def kernel(x_prompt, x_sample, p_prompt, p_sample, norm_ff1, ff1_w1, ff1_w3, ff1_w2, norm_mix, w_in, mla_q_norm, mla_kv_norm, mla_w_uq, mla_w_ukv, diff_lambda, diff_subln, dn_conv, dn_a_log, dn_dt_bias, dn_out_norm, w_branch, w_gate, w_out, norm_ff2, ff2_w1, ff2_w3, ff2_w2, norm_ple, ple_gate, ple_proj, norm_final):
    w = dict(norm_ff1=norm_ff1, ff1_w1=ff1_w1, ff1_w3=ff1_w3, ff1_w2=ff1_w2, norm_mix=norm_mix, w_in=w_in,
             mla_q_norm=mla_q_norm, mla_kv_norm=mla_kv_norm, mla_w_uq=mla_w_uq, mla_w_ukv=mla_w_ukv,
             diff_lambda=diff_lambda, diff_subln=diff_subln, dn_conv=dn_conv, dn_a_log=dn_a_log,
             dn_dt_bias=dn_dt_bias, dn_out_norm=dn_out_norm, w_branch=w_branch, w_gate=w_gate, w_out=w_out,
             norm_ff2=norm_ff2, ff2_w1=ff2_w1, ff2_w3=ff2_w3, ff2_w2=ff2_w2, norm_ple=norm_ple,
             ple_gate=ple_gate, ple_proj=ple_proj)
    depth = w_in.shape[0]
    BP, SP, _ = x_prompt.shape
    BS, SS, _ = x_sample.shape
    TP, TS = BP * SP, BS * SS
    T = TP + TS
    groups = ((BP, SP, 0), (BS, SS, TP))
    tm = min(512, SP, SS)
    assert TP % tm == 0 and TS % tm == 0 and SP % tm == 0 and SS % tm == 0

    xs = [x_prompt.reshape(TP, D_MODEL), x_sample.reshape(TS, D_MODEL)]
    ps = [p_prompt.reshape(depth, TP, PLE_DIM), p_sample.reshape(depth, TS, PLE_DIM)]
    npt = TP // tm

    tab = _rope_tables(max(SP, SS))
    masks = _dn_masks(2)
    bd_bf = _dn_masks(DN_HEADS)[4].astype(bf16)
    ones_pat = jnp.tile(jnp.concatenate([jnp.zeros((1, 64), f32), jnp.ones((1, 64), f32)], 1), (1, 4))
    row = lambda v: v.reshape(1, -1).astype(f32)
    mla_units = tuple((h, h, h) for h in range(MLA_HEADS))
    diff_units = tuple((hm, hm // 4, hm // 2) for hm in range(2 * DIFF_HEADS))
    no_sc = jnp.zeros((2,), f32)
    no_g = jnp.zeros((1, LANES), f32)

    for li in range(depth):
        lw = _layer_weights(w, li)
        x = _ffn(xs, row(norm_ff1[li]), lw['ff1_w1'], lw['ff1_w3'], lw['ff1_w2'], n_prompt_tiles=npt, tm=tm)

        (mq, mk, mv, dq, dk, dv, nqkv, ng, nz, lq, lk, lv) = _inproj(
            x, row(norm_mix[li]), lw['w_in'], row(mla_q_norm[li]), row(mla_kv_norm[li]),
            lw['wuq'], lw['wuk'], lw['wuv'], ones_pat, tab, TP // tm, SP // tm, SS // tm, tm)

        lf = diff_lambda[li].astype(f32)
        lambda_init = 0.8 - 0.6 * math.exp(-0.3 * li)
        lam = jnp.exp(jnp.sum(lf[0] * lf[1])) - jnp.exp(jnp.sum(lf[2] * lf[3])) + lambda_init
        sc = jnp.stack([lam, jnp.asarray(1.0 - lambda_init, f32)]).astype(f32)
        subln = jnp.tile(row(diff_subln[li]), (1, 2))
        ya, yb = [], []
        for (B, S, off) in groups:
            tq = _seq_tile(S, 2048)
            ya.append(_flash(mq, mk, mv, no_sc, no_g, B=B, S=S, off=off, units=mla_units, diff=False,
                             tq=tq, tk=_seq_tile(S, 2048), name="mla_attn"))
            yb.append(_flash(dq, dk, dv, sc, subln, B=B, S=S, off=off, units=diff_units, diff=True,
                             tq=tq, tk=_seq_tile(S, 256), name="diff_attn"))
        ya, yb = jnp.concatenate(ya, 0), jnp.concatenate(yb, 0)

        conv_w = jnp.concatenate([dn_conv[li].astype(f32), jnp.zeros((8 - DN_CONV, DN_QKV), f32)], 0)
        nq, nk, nv = _dn_prep(nqkv, conv_w, bd_bf, TP // tm, SP // tm, SS // tm, tm)
        gates = ng[:, :16].reshape(T // DN_CHUNK, DN_CHUNK, 4, DN_HEADS)
        gates = jnp.transpose(gates, (0, 2, 3, 1)).reshape(T // DN_CHUNK, 4, 4 * DN_CHUNK)
        gates = jnp.concatenate([gates, jnp.zeros_like(gates)], 1)
        al, dtb = dn_a_log[li].astype(f32), dn_dt_bias[li].astype(f32)
        rep = lambda v: jnp.repeat(v, DN_CHUNK)[None, :]
        prm = jnp.concatenate([rep(jnp.exp(al[0])), rep(dtb[0]), rep(jnp.exp(al[1])), rep(dtb[1]),
                               jnp.zeros((4, 4 * DN_CHUNK), f32)], 0)
        of, ob = [], []
        for (B, S, off) in groups:
            f_, b_ = _dn_chunk(nq, nk, nv, gates, prm, masks, B=B, S=S, off=off)
            of.append(f_)
            ob.append(b_)
        of, ob = jnp.concatenate(of, 0), jnp.concatenate(ob, 0)

        ts = min(1024, SP, SS)
        yd = _dilated(lq, lk, lv, TP // ts, SP // ts, SS // ts, ts)

        x = _merge(x, row(norm_mix[li]), ya, yb, of, ob, nz, jnp.tile(row(dn_out_norm[li]), (1, 4)), bd_bf,
                   yd, lw['w_gate'], lw['w_branch'], lw['w_out'], tm=tm)
        last = li == depth - 1
        ple = (ps, li, row(norm_ple[li]), lw['ple_gate'], lw['ple_proj'], row(norm_final))
        x = _ffn([x], row(norm_ff2[li]), lw['ff2_w1'], lw['ff2_w3'], lw['ff2_w2'], n_prompt_tiles=npt, tm=tm,
                 ple=ple, final=last, split_out=last)
        xs = [x]

    return (x[0].reshape(BP, SP, D_MODEL), x[1].reshape(BS, SS, D_MODEL))
```
